```python
import math
import jax, jax.numpy as jnp
from jax import lax
import numpy as np

D_MODEL = 2048
BATCH = 1
SEQ = 16384
DEPTH = 2

MEM_LEN = 256
DIFF_HEADS = 8
DIFF_HEAD_DIM = 64
DIFF_V_DIM = 2 * DIFF_HEAD_DIM
DIFF_ROT_DIM = DIFF_HEAD_DIM // 4
DIFF_QK_W = DIFF_HEADS * 2 * DIFF_HEAD_DIM
DIFF_V_W = DIFF_HEADS * DIFF_V_DIM
MLA_HEADS = 8
MLA_Q_RANK = 512
MLA_KV_RANK = 256
MLA_NOPE = 128
MLA_ROPE = 64
MLA_V = 128
MLA_W = MLA_HEADS * MLA_V
SSM_CH = 1024
SSM_GROUP = 16
SSM_GROUPS = SSM_CH // SSM_GROUP
SSM_STATE = 64
DT_MIN = 0.001
DT_MAX = 0.1
N_BRANCH = 3
BRANCH_W = 1024
X_HEADS = 4
X_HEAD_DIM = 128
X_W = X_HEADS * X_HEAD_DIM
D_FF = 5632
N_EXPERTS = 8
TOP_K = 2
N_DENSE = (DEPTH + 1) // 2
N_MOE = DEPTH // 2
ROPE_THETA = 500000.0
Q_BLOCK = 128
LN_EPS = 1e-5
RMS_EPS = 1e-6
NEG_BIG = -1e30
ALPHA = (2 * DEPTH) ** 0.25
BETA = (8 * DEPTH) ** -0.25
IN_SIZES = (DIFF_QK_W, DIFF_QK_W, DIFF_V_W, MLA_Q_RANK, MLA_KV_RANK, MLA_ROPE, SSM_CH, N_BRANCH * D_MODEL)
D_IN = sum(IN_SIZES)

kernel_name = "hybrid_gated_diffattn_mla_s5_moe_postnorm"


def layer_norm(x, g, b):
    xf = x.astype(jnp.float32)
    mu = jnp.mean(xf, axis=-1, keepdims=True)
    var = jnp.mean(jnp.square(xf - mu), axis=-1, keepdims=True)
    y = (xf - mu) * lax.rsqrt(var + LN_EPS) * g.astype(jnp.float32) + b.astype(jnp.float32)
    return y.astype(x.dtype)


def rms_norm(x, g):
    xf = x.astype(jnp.float32)
    y = xf * lax.rsqrt(jnp.mean(jnp.square(xf), axis=-1, keepdims=True) + RMS_EPS) * g.astype(jnp.float32)
    return y.astype(x.dtype)


def rope(x, positions):
    d = x.shape[-1]
    inv_freq = jnp.power(jnp.float32(ROPE_THETA), -jnp.arange(0, d, 2, dtype=jnp.float32) / d)
    ang = positions.astype(jnp.float32)[:, :, None, None] * inv_freq
    cos, sin = jnp.cos(ang), jnp.sin(ang)
    x1, x2 = jnp.split(x.astype(jnp.float32), 2, axis=-1)
    return jnp.concatenate([x1 * cos - x2 * sin, x2 * cos + x1 * sin], axis=-1).astype(x.dtype)


def partial_rope(x, positions, rot_dim):
    return jnp.concatenate([rope(x[..., :rot_dim], positions), x[..., rot_dim:]], axis=-1)


def _merge_blocks(out):
    nb, b, q, h, d = out.shape
    return jnp.moveaxis(out, 0, 1).reshape(b, nb * q, h, d)


def diff_attention(q, k, v, lam):
    seq = q.shape[1]
    scale = q.shape[-1] ** -0.5
    k_idx = jnp.arange(seq)

    def block(i):
        start = i * Q_BLOCK
        qb = lax.dynamic_slice_in_dim(q, start, Q_BLOCK, axis=1)
        s = jnp.einsum('bqhmd,bkhmd->bhmqk', qb, k).astype(jnp.float32) * scale
        mask = (start + jnp.arange(Q_BLOCK))[:, None] >= k_idx[None, :]
        p = jax.nn.softmax(jnp.where(mask, s, NEG_BIG), axis=-1)
        w = p[:, :, 0] - lam * p[:, :, 1]
        return jnp.einsum('bhqk,bkhd->bqhd', w.astype(v.dtype), v)

    return _merge_blocks(lax.map(block, jnp.arange(seq // Q_BLOCK)))


def causal_attention(q, k, v):
    seq = q.shape[1]
    scale = q.shape[-1] ** -0.5
    k_idx = jnp.arange(seq)

    def block(i):
        start = i * Q_BLOCK
        qb = lax.dynamic_slice_in_dim(q, start, Q_BLOCK, axis=1)
        s = jnp.einsum('bqhd,bkhd->bhqk', qb, k).astype(jnp.float32) * scale
        mask = (start + jnp.arange(Q_BLOCK))[:, None] >= k_idx[None, :]
        p = jax.nn.softmax(jnp.where(mask, s, NEG_BIG), axis=-1)
        return jnp.einsum('bhqk,bkhd->bqhd', p.astype(v.dtype), v)

    return _merge_blocks(lax.map(block, jnp.arange(seq // Q_BLOCK)))


def s5_ssm(u, lam_re, lam_im, b_re, b_im, c_re, c_im, d_skip, log_dt):
    bsz, seq, _ = u.shape
    f32 = jnp.float32
    uf = u.astype(f32).reshape(bsz, seq, SSM_GROUPS, SSM_GROUP)
    lam = lax.complex(lam_re.astype(f32), lam_im.astype(f32))
    dt = jnp.exp(log_dt.astype(f32))[:, None]
    lam_bar = jnp.exp(lam * dt)
    b = lax.complex(b_re.astype(f32), b_im.astype(f32))
    b_bar = ((lam_bar - 1.0) / lam)[..., None] * b
    bu = jnp.einsum('gnp,bsgp->bsgn', b_bar, uf.astype(jnp.complex64))
    a = jnp.broadcast_to(lam_bar, bu.shape)

    def combine(left, right):
        a_l, b_l = left
        a_r, b_r = right
        return a_r * a_l, a_r * b_l + b_r

    _, states = lax.associative_scan(combine, (a, bu), axis=1)
    c = lax.complex(c_re.astype(f32), c_im.astype(f32))
    y = jnp.einsum('gpn,bsgn->bsgp', c, states).real.reshape(bsz, seq, SSM_CH)
    y = y + d_skip.astype(f32) * u.astype(f32)
    return y.astype(u.dtype)


def token_mixers(h, positions, layer, w_in, w_q_up, w_kv_up, q_norm_g, kv_norm_g, diff_lambda,
                 diff_subln_g, ssm_lam_re, ssm_lam_im, ssm_b_re, ssm_b_im, ssm_c_re, ssm_c_im,
                 ssm_d, ssm_log_dt, ssm_glu_w, w_branch, w_out):
    bsz, seq, _ = h.shape
    points = np.cumsum(IN_SIZES)[:-1].tolist()
    dq, dk, dv, mq, mkv, mkr, su, gate_logits = jnp.split(h @ w_in, points, axis=-1)

    lam_init = 0.8 - 0.6 * math.exp(-0.3 * layer)
    lf = diff_lambda.astype(jnp.float32)
    lam = jnp.exp(jnp.sum(lf[0] * lf[1])) - jnp.exp(jnp.sum(lf[2] * lf[3])) + lam_init
    q = partial_rope(dq.reshape(bsz, seq, DIFF_HEADS * 2, DIFF_HEAD_DIM), positions, DIFF_ROT_DIM)
    k = partial_rope(dk.reshape(bsz, seq, DIFF_HEADS * 2, DIFF_HEAD_DIM), positions, DIFF_ROT_DIM)
    q = q.reshape(bsz, seq, DIFF_HEADS, 2, DIFF_HEAD_DIM)
    k = k.reshape(bsz, seq, DIFF_HEADS, 2, DIFF_HEAD_DIM)
    v = dv.reshape(bsz, seq, DIFF_HEADS, DIFF_V_DIM)
    o = rms_norm(diff_attention(q, k, v, lam), diff_subln_g) * (1.0 - lam_init)
    o_diff = o.reshape(bsz, seq, DIFF_V_W)

    c_q = rms_norm(mq, q_norm_g)
    qm = (c_q @ w_q_up).reshape(bsz, seq, MLA_HEADS, MLA_NOPE + MLA_ROPE)
    q_nope, q_rot = jnp.split(qm, [MLA_NOPE], axis=-1)
    qm = jnp.concatenate([q_nope, rope(q_rot, positions)], axis=-1)
    c_kv = rms_norm(mkv, kv_norm_g)
    kvm = (c_kv @ w_kv_up).reshape(bsz, seq, MLA_HEADS, MLA_NOPE + MLA_V)
    k_nope, vm = jnp.split(kvm, [MLA_NOPE], axis=-1)
    k_rot = rope(mkr.reshape(bsz, seq, 1, MLA_ROPE), positions)
    km = jnp.concatenate([k_nope, jnp.broadcast_to(k_rot, (bsz, seq, MLA_HEADS, MLA_ROPE))], axis=-1)
    o_mla = causal_attention(qm, km, vm).reshape(bsz, seq, MLA_W)

    y = jax.nn.gelu(s5_ssm(su, ssm_lam_re, ssm_lam_im, ssm_b_re, ssm_b_im, ssm_c_re, ssm_c_im,
                           ssm_d, ssm_log_dt))
    y_lin, y_gate = jnp.split(y @ ssm_glu_w, 2, axis=-1)
    o_ssm = y_lin * jax.nn.sigmoid(y_gate)

    branches = jnp.stack([o_diff, o_mla, o_ssm], axis=2)
    proj = jnp.einsum('bsnc,ncd->bsnd', branches, w_branch)
    gates = jax.nn.sigmoid(gate_logits.reshape(bsz, seq, N_BRANCH, D_MODEL))
    return jnp.sum(gates * proj, axis=2) @ w_out


def memory_cross_attention(h, mem, w_xq, w_xkv, w_xo):
    bsz, seq, _ = h.shape
    q = (h @ w_xq).reshape(bsz, seq, X_HEADS, X_HEAD_DIM)
    k, v = jnp.split(mem @ w_xkv, 2, axis=-1)
    k = k.reshape(bsz, -1, X_HEADS, X_HEAD_DIM)
    v = v.reshape(bsz, -1, X_HEADS, X_HEAD_DIM)
    s = jnp.einsum('bqhd,bkhd->bhqk', q, k).astype(jnp.float32) * X_HEAD_DIM ** -0.5
    p = jax.nn.softmax(s, axis=-1)
    o = jnp.einsum('bhqk,bkhd->bqhd', p.astype(v.dtype), v).reshape(bsz, seq, X_W)
    return o @ w_xo


def swiglu(h, w_gate_up, w_down):
    g, u = jnp.split(h @ w_gate_up, 2, axis=-1)
    return (jax.nn.silu(g) * u) @ w_down


def moe_swiglu(h, w_router, w_gate_up, w_down):
    logits = (h @ w_router).astype(jnp.float32)
    top_vals, top_idx = lax.top_k(logits, TOP_K)
    top_w = jax.nn.softmax(top_vals, axis=-1)
    combine = jnp.sum(jax.nn.one_hot(top_idx, N_EXPERTS, dtype=jnp.float32) * top_w[..., None], axis=-2)
    combine = combine.astype(h.dtype)
    out = jnp.zeros_like(h)
    for e in range(N_EXPERTS):
        out = out + combine[..., e:e + 1] * swiglu(h, w_gate_up[e], w_down[e])
    return out


def setup_inputs(seed: int = 0) -> dict:
    key = jax.random.key(seed)
    ks = iter(jax.random.split(key, 48))
    f32 = jnp.float32

    def nrm(shape, scale):
        return jax.random.normal(next(ks), shape, f32) * scale

    def gain(shape):
        return 1.0 + nrm(shape, 0.02)

    L, G, N, P = DEPTH, SSM_GROUPS, SSM_STATE, SSM_GROUP
    x = nrm((BATCH, SEQ, D_MODEL), 1.0)
    mem = nrm((BATCH, MEM_LEN, D_MODEL), 1.0)
    positions = (jnp.arange(SEQ, dtype=jnp.int32)[None, :]
                 + jax.random.randint(next(ks), (BATCH, 1), 0, 4096, dtype=jnp.int32))
    log_dt = jax.random.uniform(next(ks), (L, G), f32, math.log(DT_MIN), math.log(DT_MAX))
    lam_im = jnp.pi * jnp.arange(N, dtype=f32)[None, None, :] + nrm((L, G, N), 0.01)
    return {
        "x": x,
        "mem": mem,
        "positions": positions,
        "w_in": nrm((L, D_MODEL, D_IN), D_MODEL ** -0.5),
        "w_q_up": nrm((L, MLA_Q_RANK, MLA_HEADS * (MLA_NOPE + MLA_ROPE)), MLA_Q_RANK ** -0.5),
        "w_kv_up": nrm((L, MLA_KV_RANK, MLA_HEADS * (MLA_NOPE + MLA_V)), MLA_KV_RANK ** -0.5),
        "q_norm_g": gain((L, MLA_Q_RANK)),
        "kv_norm_g": gain((L, MLA_KV_RANK)),
        "diff_lambda": nrm((L, 4, DIFF_HEAD_DIM), 0.1),
        "diff_subln_g": gain((L, DIFF_V_DIM)),
        "ssm_lam_re": -0.5 + nrm((L, G, N), 0.01),
        "ssm_lam_im": lam_im,
        "ssm_b_re": nrm((L, G, N, P), (2 * P) ** -0.5),
        "ssm_b_im": nrm((L, G, N, P), (2 * P) ** -0.5),
        "ssm_c_re": nrm((L, G, P, N), (2 * N) ** -0.5),
        "ssm_c_im": nrm((L, G, P, N), (2 * N) ** -0.5),
        "ssm_d": nrm((L, SSM_CH), 1.0),
        "ssm_log_dt": log_dt,
        "ssm_glu_w": nrm((L, SSM_CH, 2 * SSM_CH), SSM_CH ** -0.5),
        "w_branch": nrm((L, N_BRANCH, BRANCH_W, D_MODEL), BRANCH_W ** -0.5),
        "w_out": nrm((L, D_MODEL, D_MODEL), BETA * D_MODEL ** -0.5),
        "ln_mix_g": gain((L, D_MODEL)),
        "ln_mix_b": nrm((L, D_MODEL), 0.02),
        "w_xq": nrm((L, D_MODEL, X_W), D_MODEL ** -0.5),
        "w_xkv": nrm((L, D_MODEL, 2 * X_W), D_MODEL ** -0.5),
        "w_xo": nrm((L, X_W, D_MODEL), BETA * X_W ** -0.5),
        "ln_x_g": gain((L, D_MODEL)),
        "ln_x_b": nrm((L, D_MODEL), 0.02),
        "ffn_w_gate_up": nrm((N_DENSE, D_MODEL, 2 * D_FF), D_MODEL ** -0.5),
        "ffn_w_down": nrm((N_DENSE, D_FF, D_MODEL), BETA * D_FF ** -0.5),
        "moe_router": nrm((N_MOE, D_MODEL, N_EXPERTS), D_MODEL ** -0.5),
        "moe_w_gate_up": nrm((N_MOE, N_EXPERTS, D_MODEL, 2 * D_FF), D_MODEL ** -0.5),
        "moe_w_down": nrm((N_MOE, N_EXPERTS, D_FF, D_MODEL), BETA * D_FF ** -0.5),
        "ln_ffn_g": gain((L, D_MODEL)),
        "ln_ffn_b": nrm((L, D_MODEL), 0.02),
    }


def reference(x, mem, positions, w_in, w_q_up, w_kv_up, q_norm_g, kv_norm_g, diff_lambda,
              diff_subln_g, ssm_lam_re, ssm_lam_im, ssm_b_re, ssm_b_im, ssm_c_re, ssm_c_im, ssm_d,
              ssm_log_dt, ssm_glu_w, w_branch, w_out, ln_mix_g, ln_mix_b, w_xq, w_xkv, w_xo,
              ln_x_g, ln_x_b, ffn_w_gate_up, ffn_w_down, moe_router, moe_w_gate_up, moe_w_down,
              ln_ffn_g, ln_ffn_b):
    for l in range(DEPTH):
        mix = token_mixers(x, positions, l, w_in[l], w_q_up[l], w_kv_up[l], q_norm_g[l],
                           kv_norm_g[l], diff_lambda[l], diff_subln_g[l], ssm_lam_re[l],
                           ssm_lam_im[l], ssm_b_re[l], ssm_b_im[l], ssm_c_re[l], ssm_c_im[l],
                           ssm_d[l], ssm_log_dt[l], ssm_glu_w[l], w_branch[l], w_out[l])
        x = layer_norm(ALPHA * x + mix, ln_mix_g[l], ln_mix_b[l])
        xa = memory_cross_attention(x, mem, w_xq[l], w_xkv[l], w_xo[l])
        x = layer_norm(ALPHA * x + xa, ln_x_g[l], ln_x_b[l])
        if l % 2 == 0:
            f = swiglu(x, ffn_w_gate_up[l // 2], ffn_w_down[l // 2])
        else:
            f = moe_swiglu(x, moe_router[l // 2], moe_w_gate_up[l // 2], moe_w_down[l // 2])
        x = layer_norm(ALPHA * x + f, ln_ffn_g[l], ln_ffn_b[l])
    return x
```

```python
import functools
import math

import jax
import jax.numpy as jnp
from jax import lax
from jax.experimental import pallas as pl
from jax.experimental.pallas import tpu as pltpu

D_MODEL = 2048
DEPTH = 2
MEM_LEN = 256
DIFF_HEADS = 8
DIFF_HEAD_DIM = 64
DIFF_V_DIM = 128
DIFF_ROT_DIM = 16
DIFF_QK_W = 1024
DIFF_V_W = 1024
MLA_HEADS = 8
MLA_Q_RANK = 512
MLA_KV_RANK = 256
MLA_NOPE = 128
MLA_ROPE = 64
MLA_V = 128
MLA_QK_PAD = 256
SSM_CH = 1024
SSM_GROUP = 16
SSM_GROUPS = 64
SSM_STATE = 64
SSM_SEGS = 8
SSM_BLOCKS = 8
SSM_BLOCK_STATE = 512
N_BRANCH = 3
BRANCH_W = 1024
X_HEADS = 4
X_HEAD_DIM = 128
X_W = 512
D_FF = 5632
N_EXPERTS = 8
ROPE_THETA = 500000.0
LN_EPS = 1e-5
RMS_EPS = 1e-6
NEG_BIG = -1e30
ALPHA = (2 * DEPTH) ** 0.25

A_DQ, A_DK, A_DV, A_MQ, A_MKV, A_SU, A_MKR, A_END = 0, 1024, 2048, 3072, 3584, 3840, 4864, 4992

VMEM_LIMIT = 56 * 1024 * 1024
F32 = jnp.float32
BF16 = jnp.bfloat16
HI = lax.Precision.HIGHEST


def _cparams(sem):
    return pltpu.CompilerParams(dimension_semantics=sem, vmem_limit_bytes=VMEM_LIMIT)


def _layer_norm(z, g, b):
    mu = jnp.mean(z, axis=-1, keepdims=True)
    zc = z - mu
    var = jnp.mean(zc * zc, axis=-1, keepdims=True)
    return zc * lax.rsqrt(var + LN_EPS) * g + b


def _rms_norm(z, g):
    return z * lax.rsqrt(jnp.mean(z * z, axis=-1, keepdims=True) + RMS_EPS) * g


def _dot(a, b):
    return jnp.dot(a, b, preferred_element_type=F32)


def _dot_nt(a, b):
    return lax.dot_general(a, b, (((1,), (1,)), ((), ())), preferred_element_type=F32)


def _inproj_kernel(x_ref, wa_ref, wqup_ref, wkn_ref, wv_ref, qng_ref, kvng_ref,
                   cd_ref, sda_ref, sdb_ref, cm_ref, sma_ref, smb_ref,
                   qd_ref, kd_ref, vd_ref, qm_ref, km_ref, vm_ref, su_ref):
    xb = x_ref[...].astype(BF16)
    h = _dot(xb, wa_ref[...])
    cd, sda, sdb = cd_ref[...], sda_ref[...], sdb_ref[...]
    cm, sma, smb = cm_ref[...], sma_ref[...], smb_ref[...]

    def rope_d(blk):
        return blk * cd + pltpu.roll(blk, 8, 1) * sda + pltpu.roll(blk, 120, 1) * sdb

    def rope_m(blk):
        return blk * cm + pltpu.roll(blk, 32, 1) * sma + pltpu.roll(blk, 96, 1) * smb

    q_scale = DIFF_HEAD_DIM ** -0.5
    for c in range(DIFF_QK_W // 128):
        lo, hi = c * 128, (c + 1) * 128
        qd_ref[:, lo:hi] = (rope_d(h[:, A_DQ + lo:A_DQ + hi]) * q_scale).astype(BF16)
        kd_ref[:, lo:hi] = rope_d(h[:, A_DK + lo:A_DK + hi]).astype(BF16)
    vd_ref[...] = h[:, A_DV:A_MQ].astype(BF16)
    su_ref[...] = h[:, A_SU:A_MKR]

    cq = _rms_norm(h[:, A_MQ:A_MKV], qng_ref[...]).astype(BF16)
    qm = _dot(cq, wqup_ref[...]) * ((MLA_NOPE + MLA_ROPE) ** -0.5)
    ckv = _rms_norm(h[:, A_MKV:A_SU], kvng_ref[...]).astype(BF16)
    kn = _dot(ckv, wkn_ref[...])
    vm_ref[...] = _dot(ckv, wv_ref[...]).astype(BF16)
    krot = rope_m(h[:, A_MKR:A_END]).astype(BF16)
    for hh in range(MLA_HEADS):
        b0 = hh * MLA_QK_PAD
        qm_ref[:, b0:b0 + 128] = qm[:, b0:b0 + 128].astype(BF16)
        qm_ref[:, b0 + 128:b0 + 256] = rope_m(qm[:, b0 + 128:b0 + 256]).astype(BF16)
        km_ref[:, b0:b0 + 128] = kn[:, hh * 128:(hh + 1) * 128].astype(BF16)
        km_ref[:, b0 + 128:b0 + 256] = krot


def _inproj(x, wa, wqup, wkn, wv, qng, kvng, tabs, tm=256):
    s = x.shape[0]
    row = lambda w: pl.BlockSpec((tm, w), lambda i: (i, 0))
    const = lambda a: pl.BlockSpec(a.shape, lambda i: (0,) * a.ndim, pipeline_mode=pl.Buffered(1))
    out_shapes = [jax.ShapeDtypeStruct((s, 1024), BF16), jax.ShapeDtypeStruct((s, 1024), BF16),
                  jax.ShapeDtypeStruct((s, 1024), BF16), jax.ShapeDtypeStruct((s, 2048), BF16),
                  jax.ShapeDtypeStruct((s, 2048), BF16), jax.ShapeDtypeStruct((s, 1024), BF16),
                  jax.ShapeDtypeStruct((s, 1024), F32)]
    return pl.pallas_call(
        _inproj_kernel,
        grid=(s // tm,),
        in_specs=[row(D_MODEL), const(wa), const(wqup), const(wkn), const(wv), const(qng), const(kvng)]
                 + [row(128)] * 6,
        out_specs=[row(1024), row(1024), row(1024), row(2048), row(2048), row(1024), row(1024)],
        out_shape=out_shapes,
        compiler_params=_cparams(("parallel",)),
        name="inproj",
    )(x, wa, wqup, wkn, wv, qng, kvng, *tabs)


def _attn_kernel(lam_ref, q_ref, k_ref, v_ref, g_ref, o_ref, m_sc, l_sc, acc_sc, *, tq, diff, post_scale):
    qi = pl.program_id(1)
    q = q_ref[...]
    if diff:
        lane = lax.broadcasted_iota(jnp.int32, q.shape, 1)
        zero = jnp.zeros_like(q)
        q = jnp.concatenate([jnp.where(lane < DIFF_HEAD_DIM, q, zero),
                             jnp.where(lane >= DIFF_HEAD_DIM, q, zero)], axis=0)
    rows = q.shape[0]
    m_sc[...] = jnp.full(m_sc.shape, NEG_BIG, F32)
    l_sc[...] = jnp.zeros(l_sc.shape, F32)
    acc_sc[...] = jnp.zeros(acc_sc.shape, F32)

    def step(j, masked):
        off = pl.multiple_of(j * tq, tq)
        s = _dot_nt(q, k_ref[pl.ds(off, tq), :])
        if masked:
            r = lax.broadcasted_iota(jnp.int32, s.shape, 0)
            if diff:
                r = jnp.where(r >= tq, r - tq, r)
            c = lax.broadcasted_iota(jnp.int32, s.shape, 1)
            s = jnp.where(r >= c, s, NEG_BIG)
        m_old = m_sc[...]
        m_new = jnp.maximum(m_old, jnp.max(s, axis=1, keepdims=True))
        a = jnp.exp(m_old - m_new)
        p = jnp.exp(s - m_new)
        l_sc[...] = a * l_sc[...] + jnp.sum(p, axis=1, keepdims=True)
        acc_sc[...] = a * acc_sc[...] + _dot(p.astype(BF16), v_ref[pl.ds(off, tq), :])
        m_sc[...] = m_new

    def body(j, carry):
        step(j, False)
        return carry

    lax.fori_loop(0, qi, body, 0)
    step(qi, True)
    o = acc_sc[...] / l_sc[...]
    if diff:
        o = o[:tq] - lam_ref[0] * o[tq:]
        o = _rms_norm(o, g_ref[...]) * post_scale
    o_ref[...] = o.astype(o_ref.dtype)


def _attention(q, k, v, lam, g, *, heads, dqk, diff, post_scale=1.0, tq=512):
    s = q.shape[0]
    rows = 2 * tq if diff else tq
    kern = functools.partial(_attn_kernel, tq=tq, diff=diff, post_scale=post_scale)
    return pl.pallas_call(
        kern,
        grid=(heads, s // tq),
        in_specs=[pl.BlockSpec(memory_space=pltpu.SMEM),
                  pl.BlockSpec((tq, dqk), lambda h, i: (i, h)),
                  pl.BlockSpec((s, dqk), lambda h, i: (0, h)),
                  pl.BlockSpec((s, 128), lambda h, i: (0, h)),
                  pl.BlockSpec((1, 128), lambda h, i: (0, 0))],
        out_specs=pl.BlockSpec((tq, 128), lambda h, i: (i, h)),
        out_shape=jax.ShapeDtypeStruct((s, heads * 128), BF16),
        scratch_shapes=[pltpu.VMEM((rows, 1), F32), pltpu.VMEM((rows, 1), F32),
                        pltpu.VMEM((rows, 128), F32)],
        compiler_params=_cparams(("parallel", "parallel")),
        name="diff_attn" if diff else "mla_attn",
    )(lam, q, k, v, g)


def _ssm_kernel(u_ref, wbr_ref, wbi_ref, ar_ref, ai_ref, wcr_ref, wci_ref, d_ref, x0r_ref, x0i_ref,
                y_ref, xer_ref, xei_ref, br_sc, bi_sc, sr_sc, si_sc, *, tc):
    i = pl.program_id(1)

    @pl.when(i == 0)
    def _():
        sr_sc[...] = x0r_ref[0]
        si_sc[...] = x0i_ref[0]

    u = u_ref[...]
    br_sc[...] = jnp.dot(u, wbr_ref[0], precision=HI, preferred_element_type=F32)
    bi_sc[...] = jnp.dot(u, wbi_ref[0], precision=HI, preferred_element_type=F32)
    ar = jnp.broadcast_to(ar_ref[0], (SSM_SEGS, SSM_BLOCK_STATE))
    ai = jnp.broadcast_to(ai_ref[0], (SSM_SEGS, SSM_BLOCK_STATE))

    def body(t, carry):
        re, im = carry
        off = pl.multiple_of(t * SSM_SEGS, SSM_SEGS)
        nre = ar * re - ai * im + br_sc[pl.ds(off, SSM_SEGS), :]
        nim = ar * im + ai * re + bi_sc[pl.ds(off, SSM_SEGS), :]
        br_sc[pl.ds(off, SSM_SEGS), :] = nre
        bi_sc[pl.ds(off, SSM_SEGS), :] = nim
        return nre, nim

    re, im = lax.fori_loop(0, tc, body, (sr_sc[...], si_sc[...]), unroll=8)
    sr_sc[...] = re
    si_sc[...] = im
    y = (jnp.dot(br_sc[...], wcr_ref[0], precision=HI, preferred_element_type=F32)
         - jnp.dot(bi_sc[...], wci_ref[0], precision=HI, preferred_element_type=F32)
         + d_ref[...] * u)
    y_ref[...] = jax.nn.gelu(y).astype(y_ref.dtype)

    @pl.when(i == pl.num_programs(1) - 1)
    def _():
        xer_ref[0] = re
        xei_ref[0] = im


def _ssm_pass(u_perm, wbr, wbi, ar, ai, wcr, wci, d, x0r, x0i, tc):
    rows = u_perm.shape[0]
    nchunk = rows // (tc * SSM_SEGS)
    blk3 = lambda a: pl.BlockSpec((1,) + a.shape[1:], lambda j, i: (j, 0, 0))
    st_shape = jax.ShapeDtypeStruct((SSM_BLOCKS, SSM_SEGS, SSM_BLOCK_STATE), F32)
    return pl.pallas_call(
        functools.partial(_ssm_kernel, tc=tc),
        grid=(SSM_BLOCKS, nchunk),
        in_specs=[pl.BlockSpec((tc * SSM_SEGS, 128), lambda j, i: (i, j)),
                  blk3(wbr), blk3(wbi), blk3(ar), blk3(ai), blk3(wcr), blk3(wci),
                  pl.BlockSpec((1, 128), lambda j, i: (0, j)), blk3(x0r), blk3(x0i)],
        out_specs=[pl.BlockSpec((tc * SSM_SEGS, 128), lambda j, i: (i, j)),
                   pl.BlockSpec((1, SSM_SEGS, SSM_BLOCK_STATE), lambda j, i: (j, 0, 0)),
                   pl.BlockSpec((1, SSM_SEGS, SSM_BLOCK_STATE), lambda j, i: (j, 0, 0))],
        out_shape=[jax.ShapeDtypeStruct((rows, SSM_CH), BF16), st_shape, st_shape],
        scratch_shapes=[pltpu.VMEM((tc * SSM_SEGS, SSM_BLOCK_STATE), F32),
                        pltpu.VMEM((tc * SSM_SEGS, SSM_BLOCK_STATE), F32),
                        pltpu.VMEM((SSM_SEGS, SSM_BLOCK_STATE), F32),
                        pltpu.VMEM((SSM_SEGS, SSM_BLOCK_STATE), F32)],
        compiler_params=_cparams(("parallel", "arbitrary")),
        name="ssm_scan",
    )(u_perm, wbr, wbi, ar, ai, wcr, wci, d, x0r, x0i)


def _ssm_params(lam_re, lam_im, b_re, b_im, c_re, c_im, log_dt, seg_len):
    lam = lax.complex(lam_re, lam_im)
    dt = jnp.exp(log_dt)[:, None]
    lam_bar = jnp.exp(lam * dt)
    b_bar = ((lam_bar - 1.0) / lam)[..., None] * lax.complex(b_re, b_im)
    eye = jnp.eye(SSM_BLOCKS, dtype=F32)

    def in_mat(w):
        w = w.reshape(SSM_BLOCKS, 8, SSM_STATE, SSM_GROUP)
        return jnp.einsum('jgnp,gh->jgphn', w, eye).reshape(SSM_BLOCKS, 8 * SSM_GROUP, 8 * SSM_STATE)

    def out_mat(w):
        w = w.reshape(SSM_BLOCKS, 8, SSM_GROUP, SSM_STATE)
        return jnp.einsum('jgpn,gh->jgnhp', w, eye).reshape(SSM_BLOCKS, 8 * SSM_STATE, 8 * SSM_GROUP)

    vec = lambda a: a.reshape(SSM_BLOCKS, 1, SSM_BLOCK_STATE)
    lam_seg = jnp.exp(lam * dt * seg_len)
    return dict(wbr=in_mat(jnp.real(b_bar)), wbi=in_mat(jnp.imag(b_bar)),
                ar=vec(jnp.real(lam_bar)), ai=vec(jnp.imag(lam_bar)),
                wcr=out_mat(c_re), wci=out_mat(c_im),
                sr=vec(jnp.real(lam_seg)), si=vec(jnp.imag(lam_seg)))


def _ssm(su, p, d, tc):
    s = su.shape[0]
    seg_len = s // SSM_SEGS
    u_perm = su.reshape(SSM_SEGS, seg_len, SSM_CH).transpose(1, 0, 2).reshape(s, SSM_CH)
    zeros = jnp.zeros((SSM_BLOCKS, SSM_SEGS, SSM_BLOCK_STATE), F32)
    args = (u_perm, p['wbr'], p['wbi'], p['ar'], p['ai'], p['wcr'], p['wci'], d)
    _, xer, xei = _ssm_pass(*args, zeros, zeros, tc)
    x0r, x0i = [zeros[:, 0]], [zeros[:, 0]]
    for sg in range(SSM_SEGS - 1):
        pr, pi = x0r[-1], x0i[-1]
        x0r.append(p['sr'][:, 0] * pr - p['si'][:, 0] * pi + xer[:, sg])
        x0i.append(p['sr'][:, 0] * pi + p['si'][:, 0] * pr + xei[:, sg])
    y, _, _ = _ssm_pass(*args, jnp.stack(x0r, axis=1), jnp.stack(x0i, axis=1), tc)
    return y.reshape(seg_len, SSM_SEGS, SSM_CH).transpose(1, 0, 2).reshape(s, SSM_CH)


def _glu_kernel(y_ref, wl_ref, wg_ref, o_ref):
    y = y_ref[...]
    o_ref[...] = (_dot(y, wl_ref[...]) * jax.nn.sigmoid(_dot(y, wg_ref[...]))).astype(o_ref.dtype)


def _glu(y, wl, wg, tm=512):
    s = y.shape[0]
    return pl.pallas_call(
        _glu_kernel,
        grid=(s // tm,),
        in_specs=[pl.BlockSpec((tm, SSM_CH), lambda i: (i, 0)),
                  pl.BlockSpec(wl.shape, lambda i: (0, 0)),
                  pl.BlockSpec(wg.shape, lambda i: (0, 0))],
        out_specs=pl.BlockSpec((tm, SSM_CH), lambda i: (i, 0)),
        out_shape=jax.ShapeDtypeStruct((s, SSM_CH), BF16),
        compiler_params=_cparams(("parallel",)),
        name="ssm_glu",
    )(y, wl, wg)


def _merge_kernel(x_ref, od_ref, om_ref, os_ref, wg_ref, wb_ref, wo_ref, g_ref, b_ref, o_ref, xb_sc):
    j = pl.program_id(1)

    @pl.when(j == 0)
    def _():
        xb_sc[...] = x_ref[...].astype(BF16)
        o_ref[...] = jnp.zeros(o_ref.shape, F32)

    xb = xb_sc[...]
    merged = None
    for n, br_ref in enumerate((od_ref, om_ref, os_ref)):
        term = jax.nn.sigmoid(_dot(xb, wg_ref[n])) * _dot(br_ref[...], wb_ref[n])
        merged = term if merged is None else merged + term
    o_ref[...] += _dot(merged.astype(BF16), wo_ref[...])

    @pl.when(j == pl.num_programs(1) - 1)
    def _():
        o_ref[...] = _layer_norm(ALPHA * x_ref[...] + o_ref[...], g_ref[...], b_ref[...])


def _merge(x, od, om, os_, wg, wb, wo, g, b, tm=512, tn=256):
    s = x.shape[0]
    br = pl.BlockSpec((tm, BRANCH_W), lambda i, j: (i, 0))
    vec = pl.BlockSpec((1, D_MODEL), lambda i, j: (0, 0))
    return pl.pallas_call(
        _merge_kernel,
        grid=(s // tm, D_MODEL // tn),
        in_specs=[pl.BlockSpec((tm, D_MODEL), lambda i, j: (i, 0)), br, br, br,
                  pl.BlockSpec((N_BRANCH, D_MODEL, tn), lambda i, j: (0, 0, j)),
                  pl.BlockSpec((N_BRANCH, BRANCH_W, tn), lambda i, j: (0, 0, j)),
                  pl.BlockSpec((tn, D_MODEL), lambda i, j: (j, 0)), vec, vec],
        out_specs=pl.BlockSpec((tm, D_MODEL), lambda i, j: (i, 0)),
        out_shape=jax.ShapeDtypeStruct((s, D_MODEL), F32),
        scratch_shapes=[pltpu.VMEM((tm, D_MODEL), BF16)],
        compiler_params=_cparams(("parallel", "arbitrary")),
        name="merge_out_ln",
    )(x, od, om, os_, wg, wb, wo, g, b)


def _memkv_kernel(m_ref, w_ref, o_ref):
    o_ref[...] = _dot(m_ref[...].astype(BF16), w_ref[...]).astype(o_ref.dtype)


def _memkv(mem, w):
    return pl.pallas_call(
        _memkv_kernel,
        out_shape=jax.ShapeDtypeStruct((mem.shape[0], w.shape[1]), BF16),
        compiler_params=pltpu.CompilerParams(vmem_limit_bytes=VMEM_LIMIT),
        name="mem_kv",
    )(mem, w)


def _xattn_kernel(x_ref, wq_ref, kv_ref, wo_ref, g_ref, b_ref, o_ref):
    x = x_ref[...]
    q = (_dot(x.astype(BF16), wq_ref[...]) * (X_HEAD_DIM ** -0.5)).astype(BF16)
    outs = []
    for h in range(X_HEADS):
        lo, hi = h * X_HEAD_DIM, (h + 1) * X_HEAD_DIM
        s = _dot_nt(q[:, lo:hi], kv_ref[:, lo:hi])
        p = jnp.exp(s - jnp.max(s, axis=1, keepdims=True))
        o = _dot(p.astype(BF16), kv_ref[:, X_W + lo:X_W + hi])
        outs.append((o / jnp.sum(p, axis=1, keepdims=True)).astype(BF16))
    xa = _dot(jnp.concatenate(outs, axis=1), wo_ref[...])
    o_ref[...] = _layer_norm(ALPHA * x + xa, g_ref[...], b_ref[...])


def _xattn(x, wq, kv, wo, g, b, tm=512):
    s = x.shape[0]
    full = lambda a: pl.BlockSpec(a.shape, lambda i: (0, 0))
    return pl.pallas_call(
        _xattn_kernel,
        grid=(s // tm,),
        in_specs=[pl.BlockSpec((tm, D_MODEL), lambda i: (i, 0)), full(wq), full(kv), full(wo), full(g), full(b)],
        out_specs=pl.BlockSpec((tm, D_MODEL), lambda i: (i, 0)),
        out_shape=jax.ShapeDtypeStruct((s, D_MODEL), F32),
        compiler_params=_cparams(("parallel",)),
        name="xattn_ln",
    )(x, wq, kv, wo, g, b)


def _router_kernel(x_ref, w_ref, c_ref):
    logits = jnp.dot(x_ref[...], w_ref[...], precision=HI, preferred_element_type=F32)
    lane = lax.broadcasted_iota(jnp.int32, logits.shape, 1)
    logits = jnp.where(lane < N_EXPERTS, logits, NEG_BIG)
    v1 = jnp.max(logits, axis=1, keepdims=True)
    i1 = jnp.min(jnp.where(logits == v1, lane, 128), axis=1, keepdims=True)
    rest = jnp.where(lane == i1, NEG_BIG, logits)
    v2 = jnp.max(rest, axis=1, keepdims=True)
    i2 = jnp.min(jnp.where(rest == v2, lane, 128), axis=1, keepdims=True)
    e2 = jnp.exp(v2 - v1)
    w1 = 1.0 / (1.0 + e2)
    w2 = e2 / (1.0 + e2)
    c_ref[...] = jnp.where(lane == i1, w1, 0.0) + jnp.where(lane == i2, w2, 0.0)


def _router(x, w_pad, tm=512):
    s = x.shape[0]
    return pl.pallas_call(
        _router_kernel,
        grid=(s // tm,),
        in_specs=[pl.BlockSpec((tm, D_MODEL), lambda i: (i, 0)), pl.BlockSpec(w_pad.shape, lambda i: (0, 0))],
        out_specs=pl.BlockSpec((tm, 128), lambda i: (i, 0)),
        out_shape=jax.ShapeDtypeStruct((s, 128), F32),
        compiler_params=_cparams(("parallel",)),
        name="router_top2",
    )(x, w_pad)


def _ffn_kernel(x_ref, c_ref, wg_ref, wu_ref, wd_ref, g_ref, b_ref, o_ref, xb_sc, *, weighted):
    e, f = pl.program_id(1), pl.program_id(2)

    @pl.when((e == 0) & (f == 0))
    def _():
        xb_sc[...] = x_ref[...].astype(BF16)
        o_ref[...] = jnp.zeros(o_ref.shape, F32)

    xb = xb_sc[...]
    a = jax.nn.silu(_dot(xb, wg_ref[0])) * _dot(xb, wu_ref[0])
    if weighted:
        lane = lax.broadcasted_iota(jnp.int32, c_ref.shape, 1)
        a = a * jnp.sum(jnp.where(lane == e, c_ref[...], 0.0), axis=1, keepdims=True)
    o_ref[...] += _dot(a.astype(BF16), wd_ref[0])

    @pl.when((e == pl.num_programs(1) - 1) & (f == pl.num_programs(2) - 1))
    def _():
        o_ref[...] = _layer_norm(ALPHA * x_ref[...] + o_ref[...], g_ref[...], b_ref[...])


def _ffn(x, combine, wgu, wd, g, b, *, weighted, tm=512, tf=512):
    s = x.shape[0]
    ne = wgu.shape[0]
    nf = D_FF // tf
    vec = pl.BlockSpec((1, D_MODEL), lambda i, e, f: (0, 0))
    return pl.pallas_call(
        functools.partial(_ffn_kernel, weighted=weighted),
        grid=(s // tm, ne, nf),
        in_specs=[pl.BlockSpec((tm, D_MODEL), lambda i, e, f: (i, 0)),
                  pl.BlockSpec((tm, 128), lambda i, e, f: (i, 0)),
                  pl.BlockSpec((1, D_MODEL, tf), lambda i, e, f: (e, 0, f)),
                  pl.BlockSpec((1, D_MODEL, tf), lambda i, e, f: (e, 0, nf + f)),
                  pl.BlockSpec((1, tf, D_MODEL), lambda i, e, f: (e, f, 0)), vec, vec],
        out_specs=pl.BlockSpec((tm, D_MODEL), lambda i, e, f: (i, 0)),
        out_shape=jax.ShapeDtypeStruct((s, D_MODEL), F32),
        scratch_shapes=[pltpu.VMEM((tm, D_MODEL), BF16)],
        compiler_params=_cparams(("parallel", "arbitrary", "arbitrary")),
        name="ffn_ln",
    )(x, combine, wgu, wgu, wd, g, b)


def _rope_tables(positions):
    pos = positions.astype(F32)[:, None]
    lane = jnp.arange(128) % 64

    def tables(rot):
        half = rot // 2
        inv_freq = jnp.power(jnp.float32(ROPE_THETA), -jnp.arange(0, rot, 2, dtype=F32) / rot)
        ang = pos * inv_freq[None, :]
        cos = jnp.take(jnp.cos(ang), lane % half, axis=1)
        sin = jnp.take(jnp.sin(ang), lane % half, axis=1)
        first, second = (lane < half)[None, :], ((lane >= half) & (lane < rot))[None, :]
        c = jnp.where(first | second, cos, 1.0)
        s_prev = jnp.where(second, sin, 0.0)
        s_next = jnp.where(first, -sin, 0.0)
        return c, s_prev, s_next

    return tables(DIFF_ROT_DIM) + tables(MLA_ROPE)


def _pack_layer(l, w_in, w_q_up, w_kv_up, q_norm_g, kv_norm_g, w_branch, w_out, ssm_glu_w, w_xq, w_xkv, w_xo):
    wi = w_in[l]
    parts = [wi[:, 0:3072], wi[:, 3072:3584], wi[:, 3584:3840], wi[:, 3904:4928], wi[:, 3840:3904],
             jnp.zeros((D_MODEL, 64), F32)]
    wa = jnp.concatenate(parts, axis=1).astype(BF16)
    wqup = w_q_up[l].reshape(MLA_Q_RANK, MLA_HEADS, MLA_NOPE + MLA_ROPE)
    wqup = jnp.pad(wqup, ((0, 0), (0, 0), (0, MLA_QK_PAD - MLA_NOPE - MLA_ROPE)))
    wqup = wqup.reshape(MLA_Q_RANK, MLA_HEADS * MLA_QK_PAD).astype(BF16)
    wkv = w_kv_up[l].reshape(MLA_KV_RANK, MLA_HEADS, MLA_NOPE + MLA_V)
    wkn = wkv[:, :, :MLA_NOPE].reshape(MLA_KV_RANK, MLA_HEADS * MLA_NOPE).astype(BF16)
    wv = wkv[:, :, MLA_NOPE:].reshape(MLA_KV_RANK, MLA_HEADS * MLA_V).astype(BF16)
    wgate = wi[:, 4928:].reshape(D_MODEL, N_BRANCH, D_MODEL).transpose(1, 0, 2).astype(BF16)
    return dict(wa=wa, wqup=wqup, wkn=wkn, wv=wv,
                qng=q_norm_g[l][None, :], kvng=kv_norm_g[l][None, :],
                wgate=wgate, wb=w_branch[l].astype(BF16), wo=w_out[l].astype(BF16),
                wl=ssm_glu_w[l][:, :SSM_CH].astype(BF16), wgl=ssm_glu_w[l][:, SSM_CH:].astype(BF16),
                wxq=w_xq[l].astype(BF16), wxkv=w_xkv[l].astype(BF16), wxo=w_xo[l].astype(BF16))


def _forward(x, mem, positions, w_in, w_q_up, w_kv_up, q_norm_g, kv_norm_g, diff_lambda, diff_subln_g,
             ssm_lam_re, ssm_lam_im, ssm_b_re, ssm_b_im, ssm_c_re, ssm_c_im, ssm_d, ssm_log_dt, ssm_glu_w,
             w_branch, w_out, ln_mix_g, ln_mix_b, w_xq, w_xkv, w_xo, ln_x_g, ln_x_b, ffn_w_gate_up,
             ffn_w_down, moe_router, moe_w_gate_up, moe_w_down, ln_ffn_g, ln_ffn_b, *, ssm_tc, attn_tq):
    s = x.shape[0]
    tabs = _rope_tables(positions)
    row = lambda v: v[None, :]
    ones_g = jnp.ones((1, 128), F32)
    dummy_c = jnp.zeros((s, 128), F32)
    for l in range(DEPTH):
        pk = _pack_layer(l, w_in, w_q_up, w_kv_up, q_norm_g, kv_norm_g, w_branch, w_out, ssm_glu_w,
                         w_xq, w_xkv, w_xo)
        qd, kd, vd, qm, km, vm, su = _inproj(x, pk['wa'], pk['wqup'], pk['wkn'], pk['wv'], pk['qng'],
                                             pk['kvng'], tabs)
        lam_init = 0.8 - 0.6 * math.exp(-0.3 * l)
        lf = diff_lambda[l]
        lam = (jnp.exp(jnp.sum(lf[0] * lf[1])) - jnp.exp(jnp.sum(lf[2] * lf[3])) + lam_init).reshape(1)
        o_diff = _attention(qd, kd, vd, lam, row(diff_subln_g[l]), heads=DIFF_HEADS, dqk=128, diff=True,
                            post_scale=1.0 - lam_init, tq=attn_tq)
        o_mla = _attention(qm, km, vm, lam, ones_g, heads=MLA_HEADS, dqk=MLA_QK_PAD, diff=False, tq=attn_tq)
        sp = _ssm_params(ssm_lam_re[l], ssm_lam_im[l], ssm_b_re[l], ssm_b_im[l], ssm_c_re[l], ssm_c_im[l],
                         ssm_log_dt[l], s // SSM_SEGS)
        y = _ssm(su, sp, row(ssm_d[l]), ssm_tc)
        o_ssm = _glu(y, pk['wl'], pk['wgl'])
        x = _merge(x, o_diff, o_mla, o_ssm, pk['wgate'], pk['wb'], pk['wo'], row(ln_mix_g[l]), row(ln_mix_b[l]))
        kv = _memkv(mem, pk['wxkv'])
        x = _xattn(x, pk['wxq'], kv, pk['wxo'], row(ln_x_g[l]), row(ln_x_b[l]))
        if l % 2 == 0:
            x = _ffn(x, dummy_c, ffn_w_gate_up[l // 2][None].astype(BF16), ffn_w_down[l // 2][None].astype(BF16),
                     row(ln_ffn_g[l]), row(ln_ffn_b[l]), weighted=False)
        else:
            wr = jnp.pad(moe_router[l // 2], ((0, 0), (0, 128 - N_EXPERTS)))
            combine = _router(x, wr)
            x = _ffn(x, combine, moe_w_gate_up[l // 2].astype(BF16), moe_w_down[l // 2].astype(BF16),
                     row(ln_ffn_g[l]), row(ln_ffn_b[l]), weighted=True)
    return x


def kernel(x, mem, positions, w_in, w_q_up, w_kv_up, q_norm_g, kv_norm_g, diff_lambda, diff_subln_g, ssm_lam_re, ssm_lam_im, ssm_b_re, ssm_b_im, ssm_c_re, ssm_c_im, ssm_d, ssm_log_dt, ssm_glu_w, w_branch, w_out, ln_mix_g, ln_mix_b, w_xq, w_xkv, w_xo, ln_x_g, ln_x_b, ffn_w_gate_up, ffn_w_down, moe_router, moe_w_gate_up, moe_w_down, ln_ffn_g, ln_ffn_b):
    assert x.shape[0] == 1 and mem.shape[0] == 1
    seq = x.shape[1]
    out = _forward(x[0], mem[0], positions[0], w_in, w_q_up, w_kv_up, q_norm_g, kv_norm_g, diff_lambda,
                   diff_subln_g, ssm_lam_re, ssm_lam_im, ssm_b_re, ssm_b_im, ssm_c_re, ssm_c_im, ssm_d,
                   ssm_log_dt, ssm_glu_w, w_branch, w_out, ln_mix_g, ln_mix_b, w_xq, w_xkv, w_xo, ln_x_g,
                   ln_x_b, ffn_w_gate_up, ffn_w_down, moe_router, moe_w_gate_up, moe_w_down, ln_ffn_g,
                   ln_ffn_b, ssm_tc=min(256, seq // SSM_SEGS), attn_tq=min(512, seq))
    return out[None]
```

```python
import functools
import math

import jax
import jax.numpy as jnp
from jax import lax
from jax.experimental import pallas as pl
from jax.experimental.pallas import tpu as pltpu

D_MODEL = 2048
DEPTH = 2
MEM_LEN = 256
DIFF_HEADS = 8
DIFF_HEAD_DIM = 64
DIFF_V_DIM = 128
DIFF_ROT_DIM = 16
DIFF_QK_W = 1024
DIFF_V_W = 1024
MLA_HEADS = 8
MLA_Q_RANK = 512
MLA_KV_RANK = 256
MLA_NOPE = 128
MLA_ROPE = 64
MLA_V = 128
MLA_QK_PAD = 256
SSM_CH = 1024
SSM_GROUP = 16
SSM_GROUPS = 64
SSM_STATE = 64
SSM_SEGS = 8
SSM_BLOCKS = 8
SSM_BLOCK_STATE = 512
N_BRANCH = 3
BRANCH_W = 1024
X_HEADS = 4
X_HEAD_DIM = 128
X_W = 512
D_FF = 5632
N_EXPERTS = 8
ROPE_THETA = 500000.0
LN_EPS = 1e-5
RMS_EPS = 1e-6
NEG_BIG = -1e30
ALPHA = (2 * DEPTH) ** 0.25
LOG2E = math.log2(math.e)

A_DQ, A_DK, A_DV, A_MQ, A_MKV, A_SU, A_MKR, A_END = 0, 1024, 2048, 3072, 3584, 3840, 4864, 4992

VMEM_LIMIT = 56 * 1024 * 1024
F32 = jnp.float32
BF16 = jnp.bfloat16
HI = lax.Precision.HIGHEST


def _cparams(sem):
    return pltpu.CompilerParams(dimension_semantics=sem, vmem_limit_bytes=VMEM_LIMIT)


def _layer_norm(z, g, b):
    mu = jnp.mean(z, axis=-1, keepdims=True)
    zc = z - mu
    var = jnp.mean(zc * zc, axis=-1, keepdims=True)
    return zc * lax.rsqrt(var + LN_EPS) * g + b


def _rms_norm(z, g):
    return z * lax.rsqrt(jnp.mean(z * z, axis=-1, keepdims=True) + RMS_EPS) * g


def _dot(a, b):
    return jnp.dot(a, b, preferred_element_type=F32)


def _dot_nt(a, b):
    return lax.dot_general(a, b, (((1,), (1,)), ((), ())), preferred_element_type=F32)


def _inproj_kernel(x_ref, wa_ref, wqup_ref, wkn_ref, wv_ref, qng_ref, kvng_ref,
                   cd_ref, sda_ref, sdb_ref, cm_ref, sma_ref, smb_ref,
                   qd_ref, kd_ref, vd_ref, qm_ref, km_ref, vm_ref, su_ref):
    xb = x_ref[...].astype(BF16)
    h = _dot(xb, wa_ref[...])
    cd, sda, sdb = cd_ref[...], sda_ref[...], sdb_ref[...]
    cm, sma, smb = cm_ref[...], sma_ref[...], smb_ref[...]

    def rope_d(blk):
        return blk * cd + pltpu.roll(blk, 8, 1) * sda + pltpu.roll(blk, 120, 1) * sdb

    def rope_m(blk):
        return blk * cm + pltpu.roll(blk, 32, 1) * sma + pltpu.roll(blk, 96, 1) * smb

    q_scale = DIFF_HEAD_DIM ** -0.5 * LOG2E
    for c in range(DIFF_QK_W // 128):
        lo, hi = c * 128, (c + 1) * 128
        qd_ref[:, lo:hi] = (rope_d(h[:, A_DQ + lo:A_DQ + hi]) * q_scale).astype(BF16)
        kd_ref[:, lo:hi] = rope_d(h[:, A_DK + lo:A_DK + hi]).astype(BF16)
    vd_ref[...] = h[:, A_DV:A_MQ].astype(BF16)
    su_ref[...] = h[:, A_SU:A_MKR]

    cq = _rms_norm(h[:, A_MQ:A_MKV], qng_ref[...]).astype(BF16)
    qm = _dot(cq, wqup_ref[...]) * ((MLA_NOPE + MLA_ROPE) ** -0.5 * LOG2E)
    ckv = _rms_norm(h[:, A_MKV:A_SU], kvng_ref[...]).astype(BF16)
    kn = _dot(ckv, wkn_ref[...])
    vm_ref[...] = _dot(ckv, wv_ref[...]).astype(BF16)
    krot = rope_m(h[:, A_MKR:A_END]).astype(BF16)
    for hh in range(MLA_HEADS):
        b0 = hh * MLA_QK_PAD
        qm_ref[:, b0:b0 + 128] = qm[:, b0:b0 + 128].astype(BF16)
        qm_ref[:, b0 + 128:b0 + 256] = rope_m(qm[:, b0 + 128:b0 + 256]).astype(BF16)
        km_ref[:, b0:b0 + 128] = kn[:, hh * 128:(hh + 1) * 128].astype(BF16)
        km_ref[:, b0 + 128:b0 + 256] = krot


def _inproj(x, wa, wqup, wkn, wv, qng, kvng, tabs, tm=256):
    s = x.shape[0]
    row = lambda w: pl.BlockSpec((tm, w), lambda i: (i, 0))
    const = lambda a: pl.BlockSpec(a.shape, lambda i: (0,) * a.ndim, pipeline_mode=pl.Buffered(1))
    out_shapes = [jax.ShapeDtypeStruct((s, 1024), BF16), jax.ShapeDtypeStruct((s, 1024), BF16),
                  jax.ShapeDtypeStruct((s, 1024), BF16), jax.ShapeDtypeStruct((s, 2048), BF16),
                  jax.ShapeDtypeStruct((s, 2048), BF16), jax.ShapeDtypeStruct((s, 1024), BF16),
                  jax.ShapeDtypeStruct((s, 1024), F32)]
    return pl.pallas_call(
        _inproj_kernel,
        grid=(s // tm,),
        in_specs=[row(D_MODEL), const(wa), const(wqup), const(wkn), const(wv), const(qng), const(kvng)]
                 + [row(128)] * 6,
        out_specs=[row(1024), row(1024), row(1024), row(2048), row(2048), row(1024), row(1024)],
        out_shape=out_shapes,
        compiler_params=_cparams(("parallel",)),
        name="inproj",
    )(x, wa, wqup, wkn, wv, qng, kvng, *tabs)


def _col_reduce(op, s, ways=8):
    parts = s.reshape(ways, s.shape[0] // ways, s.shape[1])
    return op(op(parts, axis=0), axis=0, keepdims=True)


def _attn_kernel(lam_ref, qt_ref, k_ref, vt_ref, g_ref, o_ref, q2_sc, m_sc, l_sc, acc_sc, *,
                 tq, tk, cw, diff, post_scale):
    qi = pl.program_id(1)
    if diff:
        qt = qt_ref[...]
        feat = lax.broadcasted_iota(jnp.int32, qt.shape, 0)
        zero = jnp.zeros_like(qt)
        q2_sc[:, :tq] = jnp.where(feat < DIFF_HEAD_DIM, qt, zero)
        q2_sc[:, tq:] = jnp.where(feat >= DIFF_HEAD_DIM, qt, zero)
    else:
        q2_sc[...] = qt_ref[...]
    m_sc[...] = jnp.full(m_sc.shape, NEG_BIG, F32)
    l_sc[...] = jnp.zeros(l_sc.shape, F32)
    acc_sc[...] = jnp.zeros(acc_sc.shape, F32)
    ratio = tq // tk
    cols = q2_sc.shape[1]

    def step(j, diag):
        off = pl.multiple_of(j * tk, tk)
        kj = k_ref[pl.ds(off, tk), :]
        vj = vt_ref[j]
        q0 = lambda c: (c * cw) % tq
        blocks = [c for c in range(cols // cw) if diag is None or diag * tk < q0(c) + cw]
        qk = lambda c: _dot(kj, q2_sc[:, c * cw:(c + 1) * cw])
        s_next = qk(blocks[0])
        for n, c in enumerate(blocks):
            s = s_next
            if n + 1 < len(blocks):
                s_next = qk(blocks[n + 1])
            cs = slice(c * cw, (c + 1) * cw)
            if diag is not None:
                kpos = lax.broadcasted_iota(jnp.int32, s.shape, 0) + diag * tk
                qpos = lax.broadcasted_iota(jnp.int32, s.shape, 1) + q0(c)
                s = jnp.where(qpos >= kpos, s, NEG_BIG)
            m_old = m_sc[:, cs]
            m_new = jnp.maximum(m_old, _col_reduce(jnp.max, s))
            a = jnp.exp2(m_old - m_new)
            p = jnp.exp2(s - m_new)
            l_sc[:, cs] = a * l_sc[:, cs] + _col_reduce(jnp.sum, p)
            acc_sc[:, cs] = a * acc_sc[:, cs] + _dot(vj, p.astype(BF16))
            m_sc[:, cs] = m_new

    def body(j, carry):
        step(j, None)
        return carry

    lax.fori_loop(0, qi * ratio, body, 0)
    for d in range(ratio):
        step(qi * ratio + d, d)
    ot = acc_sc[...] / l_sc[...]
    if diff:
        ot = ot[:, :tq] - lam_ref[0] * ot[:, tq:]
    o = ot.T
    if diff:
        o = _rms_norm(o, g_ref[...]) * post_scale
    o_ref[...] = o.astype(o_ref.dtype)


def _attention(q, k, v, lam, g, *, heads, dqk, diff, post_scale=1.0, tq=512, tk=512, cw=256):
    s = q.shape[0]
    cols = 2 * tq if diff else tq
    nch = s // tk
    qt = q.T
    vt = v.reshape(nch, tk, heads, 128).transpose(2, 0, 3, 1).reshape(heads * nch, 128, tk)
    kern = functools.partial(_attn_kernel, tq=tq, tk=tk, cw=cw, diff=diff, post_scale=post_scale)
    return pl.pallas_call(
        kern,
        grid=(heads, s // tq),
        in_specs=[pl.BlockSpec(memory_space=pltpu.SMEM),
                  pl.BlockSpec((dqk, tq), lambda h, i: (h, i)),
                  pl.BlockSpec((s, dqk), lambda h, i: (0, h)),
                  pl.BlockSpec((nch, 128, tk), lambda h, i: (h, 0, 0)),
                  pl.BlockSpec((1, 128), lambda h, i: (0, 0))],
        out_specs=pl.BlockSpec((tq, 128), lambda h, i: (i, h)),
        out_shape=jax.ShapeDtypeStruct((s, heads * 128), BF16),
        scratch_shapes=[pltpu.VMEM((dqk, cols), BF16), pltpu.VMEM((1, cols), F32), pltpu.VMEM((1, cols), F32),
                        pltpu.VMEM((128, cols), F32)],
        compiler_params=_cparams(("parallel", "parallel")),
        name="diff_attn" if diff else "mla_attn",
    )(lam, qt, k, vt, g)


def _ssm_kernel(u_ref, wbr_ref, wbi_ref, ar_ref, ai_ref, wcr_ref, wci_ref, d_ref, x0r_ref, x0i_ref,
                y_ref, xer_ref, xei_ref, br_sc, bi_sc, sr_sc, si_sc, *, tc):
    i = pl.program_id(1)

    @pl.when(i == 0)
    def _():
        sr_sc[...] = x0r_ref[0]
        si_sc[...] = x0i_ref[0]

    u = u_ref[...]
    br_sc[...] = jnp.dot(u, wbr_ref[0], precision=HI, preferred_element_type=F32)
    bi_sc[...] = jnp.dot(u, wbi_ref[0], precision=HI, preferred_element_type=F32)
    ar = jnp.broadcast_to(ar_ref[0], (SSM_SEGS, SSM_BLOCK_STATE))
    ai = jnp.broadcast_to(ai_ref[0], (SSM_SEGS, SSM_BLOCK_STATE))

    def body(t, carry):
        re, im = carry
        off = pl.multiple_of(t * SSM_SEGS, SSM_SEGS)
        nre = ar * re - ai * im + br_sc[pl.ds(off, SSM_SEGS), :]
        nim = ar * im + ai * re + bi_sc[pl.ds(off, SSM_SEGS), :]
        br_sc[pl.ds(off, SSM_SEGS), :] = nre
        bi_sc[pl.ds(off, SSM_SEGS), :] = nim
        return nre, nim

    re, im = lax.fori_loop(0, tc, body, (sr_sc[...], si_sc[...]), unroll=8)
    sr_sc[...] = re
    si_sc[...] = im
    y = (jnp.dot(br_sc[...], wcr_ref[0], precision=HI, preferred_element_type=F32)
         - jnp.dot(bi_sc[...], wci_ref[0], precision=HI, preferred_element_type=F32)
         + d_ref[...] * u)
    y_ref[...] = jax.nn.gelu(y).astype(y_ref.dtype)

    @pl.when(i == pl.num_programs(1) - 1)
    def _():
        xer_ref[0] = re
        xei_ref[0] = im


def _ssm_pass(u_perm, wbr, wbi, ar, ai, wcr, wci, d, x0r, x0i, tc):
    rows = u_perm.shape[0]
    nchunk = rows // (tc * SSM_SEGS)
    blk3 = lambda a: pl.BlockSpec((1,) + a.shape[1:], lambda j, i: (j, 0, 0))
    st_shape = jax.ShapeDtypeStruct((SSM_BLOCKS, SSM_SEGS, SSM_BLOCK_STATE), F32)
    return pl.pallas_call(
        functools.partial(_ssm_kernel, tc=tc),
        grid=(SSM_BLOCKS, nchunk),
        in_specs=[pl.BlockSpec((tc * SSM_SEGS, 128), lambda j, i: (i, j)),
                  blk3(wbr), blk3(wbi), blk3(ar), blk3(ai), blk3(wcr), blk3(wci),
                  pl.BlockSpec((1, 128), lambda j, i: (0, j)), blk3(x0r), blk3(x0i)],
        out_specs=[pl.BlockSpec((tc * SSM_SEGS, 128), lambda j, i: (i, j)),
                   pl.BlockSpec((1, SSM_SEGS, SSM_BLOCK_STATE), lambda j, i: (j, 0, 0)),
                   pl.BlockSpec((1, SSM_SEGS, SSM_BLOCK_STATE), lambda j, i: (j, 0, 0))],
        out_shape=[jax.ShapeDtypeStruct((rows, SSM_CH), BF16), st_shape, st_shape],
        scratch_shapes=[pltpu.VMEM((tc * SSM_SEGS, SSM_BLOCK_STATE), F32),
                        pltpu.VMEM((tc * SSM_SEGS, SSM_BLOCK_STATE), F32),
                        pltpu.VMEM((SSM_SEGS, SSM_BLOCK_STATE), F32),
                        pltpu.VMEM((SSM_SEGS, SSM_BLOCK_STATE), F32)],
        compiler_params=_cparams(("parallel", "arbitrary")),
        name="ssm_scan",
    )(u_perm, wbr, wbi, ar, ai, wcr, wci, d, x0r, x0i)


def _ssm_params(lam_re, lam_im, b_re, b_im, c_re, c_im, log_dt, seg_len):
    lam = lax.complex(lam_re, lam_im)
    dt = jnp.exp(log_dt)[:, None]
    lam_bar = jnp.exp(lam * dt)
    b_bar = ((lam_bar - 1.0) / lam)[..., None] * lax.complex(b_re, b_im)
    eye = jnp.eye(SSM_BLOCKS, dtype=F32)

    def in_mat(w):
        w = w.reshape(SSM_BLOCKS, 8, SSM_STATE, SSM_GROUP)
        return jnp.einsum('jgnp,gh->jgphn', w, eye).reshape(SSM_BLOCKS, 8 * SSM_GROUP, 8 * SSM_STATE)

    def out_mat(w):
        w = w.reshape(SSM_BLOCKS, 8, SSM_GROUP, SSM_STATE)
        return jnp.einsum('jgpn,gh->jgnhp', w, eye).reshape(SSM_BLOCKS, 8 * SSM_STATE, 8 * SSM_GROUP)

    vec = lambda a: a.reshape(SSM_BLOCKS, 1, SSM_BLOCK_STATE)
    lam_seg = jnp.exp(lam * dt * seg_len)
    return dict(wbr=in_mat(jnp.real(b_bar)), wbi=in_mat(jnp.imag(b_bar)),
                ar=vec(jnp.real(lam_bar)), ai=vec(jnp.imag(lam_bar)),
                wcr=out_mat(c_re), wci=out_mat(c_im),
                sr=vec(jnp.real(lam_seg)), si=vec(jnp.imag(lam_seg)))


def _ssm(su, p, d, tc):
    s = su.shape[0]
    seg_len = s // SSM_SEGS
    u_perm = su.reshape(SSM_SEGS, seg_len, SSM_CH).transpose(1, 0, 2).reshape(s, SSM_CH)
    zeros = jnp.zeros((SSM_BLOCKS, SSM_SEGS, SSM_BLOCK_STATE), F32)
    args = (u_perm, p['wbr'], p['wbi'], p['ar'], p['ai'], p['wcr'], p['wci'], d)
    _, xer, xei = _ssm_pass(*args, zeros, zeros, tc)
    x0r, x0i = [zeros[:, 0]], [zeros[:, 0]]
    for sg in range(SSM_SEGS - 1):
        pr, pi = x0r[-1], x0i[-1]
        x0r.append(p['sr'][:, 0] * pr - p['si'][:, 0] * pi + xer[:, sg])
        x0i.append(p['sr'][:, 0] * pi + p['si'][:, 0] * pr + xei[:, sg])
    y, _, _ = _ssm_pass(*args, jnp.stack(x0r, axis=1), jnp.stack(x0i, axis=1), tc)
    return y.reshape(seg_len, SSM_SEGS, SSM_CH).transpose(1, 0, 2).reshape(s, SSM_CH)


def _glu_kernel(y_ref, wl_ref, wg_ref, o_ref):
    y = y_ref[...]
    o_ref[...] = (_dot(y, wl_ref[...]) * jax.nn.sigmoid(_dot(y, wg_ref[...]))).astype(o_ref.dtype)


def _glu(y, wl, wg, tm=512):
    s = y.shape[0]
    return pl.pallas_call(
        _glu_kernel,
        grid=(s // tm,),
        in_specs=[pl.BlockSpec((tm, SSM_CH), lambda i: (i, 0)),
                  pl.BlockSpec(wl.shape, lambda i: (0, 0)),
                  pl.BlockSpec(wg.shape, lambda i: (0, 0))],
        out_specs=pl.BlockSpec((tm, SSM_CH), lambda i: (i, 0)),
        out_shape=jax.ShapeDtypeStruct((s, SSM_CH), BF16),
        compiler_params=_cparams(("parallel",)),
        name="ssm_glu",
    )(y, wl, wg)


def _merge_kernel(x_ref, od_ref, om_ref, os_ref, wg_ref, wb_ref, wo_ref, g_ref, b_ref, o_ref, xb_sc):
    j = pl.program_id(1)

    @pl.when(j == 0)
    def _():
        xb_sc[...] = x_ref[...].astype(BF16)
        o_ref[...] = jnp.zeros(o_ref.shape, F32)

    xb = xb_sc[...]
    merged = None
    for n, br_ref in enumerate((od_ref, om_ref, os_ref)):
        term = jax.nn.sigmoid(_dot(xb, wg_ref[n])) * _dot(br_ref[...], wb_ref[n])
        merged = term if merged is None else merged + term
    o_ref[...] += _dot(merged.astype(BF16), wo_ref[...])

    @pl.when(j == pl.num_programs(1) - 1)
    def _():
        o_ref[...] = _layer_norm(ALPHA * x_ref[...] + o_ref[...], g_ref[...], b_ref[...])


def _merge(x, od, om, os_, wg, wb, wo, g, b, tm=512, tn=256):
    s = x.shape[0]
    br = pl.BlockSpec((tm, BRANCH_W), lambda i, j: (i, 0))
    vec = pl.BlockSpec((1, D_MODEL), lambda i, j: (0, 0))
    return pl.pallas_call(
        _merge_kernel,
        grid=(s // tm, D_MODEL // tn),
        in_specs=[pl.BlockSpec((tm, D_MODEL), lambda i, j: (i, 0)), br, br, br,
                  pl.BlockSpec((N_BRANCH, D_MODEL, tn), lambda i, j: (0, 0, j)),
                  pl.BlockSpec((N_BRANCH, BRANCH_W, tn), lambda i, j: (0, 0, j)),
                  pl.BlockSpec((tn, D_MODEL), lambda i, j: (j, 0)), vec, vec],
        out_specs=pl.BlockSpec((tm, D_MODEL), lambda i, j: (i, 0)),
        out_shape=jax.ShapeDtypeStruct((s, D_MODEL), F32),
        scratch_shapes=[pltpu.VMEM((tm, D_MODEL), BF16)],
        compiler_params=_cparams(("parallel", "arbitrary")),
        name="merge_out_ln",
    )(x, od, om, os_, wg, wb, wo, g, b)


def _memkv_kernel(m_ref, w_ref, o_ref):
    o_ref[...] = _dot(m_ref[...].astype(BF16), w_ref[...]).astype(o_ref.dtype)


def _memkv(mem, w):
    return pl.pallas_call(
        _memkv_kernel,
        out_shape=jax.ShapeDtypeStruct((mem.shape[0], w.shape[1]), BF16),
        compiler_params=pltpu.CompilerParams(vmem_limit_bytes=VMEM_LIMIT),
        name="mem_kv",
    )(mem, w)


def _xattn_kernel(x_ref, wq_ref, kv_ref, wo_ref, g_ref, b_ref, o_ref):
    x = x_ref[...]
    q = (_dot(x.astype(BF16), wq_ref[...]) * (X_HEAD_DIM ** -0.5)).astype(BF16)
    outs = []
    for h in range(X_HEADS):
        lo, hi = h * X_HEAD_DIM, (h + 1) * X_HEAD_DIM
        s = _dot_nt(q[:, lo:hi], kv_ref[:, lo:hi])
        p = jnp.exp(s - jnp.max(s, axis=1, keepdims=True))
        o = _dot(p.astype(BF16), kv_ref[:, X_W + lo:X_W + hi])
        outs.append((o / jnp.sum(p, axis=1, keepdims=True)).astype(BF16))
    xa = _dot(jnp.concatenate(outs, axis=1), wo_ref[...])
    o_ref[...] = _layer_norm(ALPHA * x + xa, g_ref[...], b_ref[...])


def _xattn(x, wq, kv, wo, g, b, tm=512):
    s = x.shape[0]
    full = lambda a: pl.BlockSpec(a.shape, lambda i: (0, 0))
    return pl.pallas_call(
        _xattn_kernel,
        grid=(s // tm,),
        in_specs=[pl.BlockSpec((tm, D_MODEL), lambda i: (i, 0)), full(wq), full(kv), full(wo), full(g), full(b)],
        out_specs=pl.BlockSpec((tm, D_MODEL), lambda i: (i, 0)),
        out_shape=jax.ShapeDtypeStruct((s, D_MODEL), F32),
        compiler_params=_cparams(("parallel",)),
        name="xattn_ln",
    )(x, wq, kv, wo, g, b)


def _router_kernel(x_ref, w_ref, c_ref):
    logits = jnp.dot(x_ref[...], w_ref[...], precision=HI, preferred_element_type=F32)
    lane = lax.broadcasted_iota(jnp.int32, logits.shape, 1)
    logits = jnp.where(lane < N_EXPERTS, logits, NEG_BIG)
    v1 = jnp.max(logits, axis=1, keepdims=True)
    i1 = jnp.min(jnp.where(logits == v1, lane, 128), axis=1, keepdims=True)
    rest = jnp.where(lane == i1, NEG_BIG, logits)
    v2 = jnp.max(rest, axis=1, keepdims=True)
    i2 = jnp.min(jnp.where(rest == v2, lane, 128), axis=1, keepdims=True)
    e2 = jnp.exp(v2 - v1)
    w1 = 1.0 / (1.0 + e2)
    w2 = e2 / (1.0 + e2)
    c_ref[...] = jnp.where(lane == i1, w1, 0.0) + jnp.where(lane == i2, w2, 0.0)


def _router(x, w_pad, tm=512):
    s = x.shape[0]
    return pl.pallas_call(
        _router_kernel,
        grid=(s // tm,),
        in_specs=[pl.BlockSpec((tm, D_MODEL), lambda i: (i, 0)), pl.BlockSpec(w_pad.shape, lambda i: (0, 0))],
        out_specs=pl.BlockSpec((tm, 128), lambda i: (i, 0)),
        out_shape=jax.ShapeDtypeStruct((s, 128), F32),
        compiler_params=_cparams(("parallel",)),
        name="router_top2",
    )(x, w_pad)


def _ffn_kernel(x_ref, c_ref, wg_ref, wu_ref, wd_ref, g_ref, b_ref, o_ref, xb_sc, *, weighted):
    e, f = pl.program_id(1), pl.program_id(2)

    @pl.when((e == 0) & (f == 0))
    def _():
        xb_sc[...] = x_ref[...].astype(BF16)
        o_ref[...] = jnp.zeros(o_ref.shape, F32)

    xb = xb_sc[...]
    a = jax.nn.silu(_dot(xb, wg_ref[0])) * _dot(xb, wu_ref[0])
    if weighted:
        lane = lax.broadcasted_iota(jnp.int32, c_ref.shape, 1)
        a = a * jnp.sum(jnp.where(lane == e, c_ref[...], 0.0), axis=1, keepdims=True)
    o_ref[...] += _dot(a.astype(BF16), wd_ref[0])

    @pl.when((e == pl.num_programs(1) - 1) & (f == pl.num_programs(2) - 1))
    def _():
        o_ref[...] = _layer_norm(ALPHA * x_ref[...] + o_ref[...], g_ref[...], b_ref[...])


def _ffn(x, combine, wgu, wd, g, b, *, weighted, tm=512, tf=512):
    s = x.shape[0]
    ne = wgu.shape[0]
    nf = D_FF // tf
    vec = pl.BlockSpec((1, D_MODEL), lambda i, e, f: (0, 0))
    return pl.pallas_call(
        functools.partial(_ffn_kernel, weighted=weighted),
        grid=(s // tm, ne, nf),
        in_specs=[pl.BlockSpec((tm, D_MODEL), lambda i, e, f: (i, 0)),
                  pl.BlockSpec((tm, 128), lambda i, e, f: (i, 0)),
                  pl.BlockSpec((1, D_MODEL, tf), lambda i, e, f: (e, 0, f)),
                  pl.BlockSpec((1, D_MODEL, tf), lambda i, e, f: (e, 0, nf + f)),
                  pl.BlockSpec((1, tf, D_MODEL), lambda i, e, f: (e, f, 0)), vec, vec],
        out_specs=pl.BlockSpec((tm, D_MODEL), lambda i, e, f: (i, 0)),
        out_shape=jax.ShapeDtypeStruct((s, D_MODEL), F32),
        scratch_shapes=[pltpu.VMEM((tm, D_MODEL), BF16)],
        compiler_params=_cparams(("parallel", "arbitrary", "arbitrary")),
        name="ffn_ln",
    )(x, combine, wgu, wgu, wd, g, b)


def _rope_tables(positions):
    pos = positions.astype(F32)[:, None]
    lane = jnp.arange(128) % 64

    def tables(rot):
        half = rot // 2
        inv_freq = jnp.power(jnp.float32(ROPE_THETA), -jnp.arange(0, rot, 2, dtype=F32) / rot)
        ang = pos * inv_freq[None, :]
        cos = jnp.take(jnp.cos(ang), lane % half, axis=1)
        sin = jnp.take(jnp.sin(ang), lane % half, axis=1)
        first, second = (lane < half)[None, :], ((lane >= half) & (lane < rot))[None, :]
        c = jnp.where(first | second, cos, 1.0)
        s_prev = jnp.where(second, sin, 0.0)
        s_next = jnp.where(first, -sin, 0.0)
        return c, s_prev, s_next

    return tables(DIFF_ROT_DIM) + tables(MLA_ROPE)


def _pack_layer(l, w_in, w_q_up, w_kv_up, q_norm_g, kv_norm_g, w_branch, w_out, ssm_glu_w, w_xq, w_xkv, w_xo):
    wi = w_in[l]
    parts = [wi[:, 0:3072], wi[:, 3072:3584], wi[:, 3584:3840], wi[:, 3904:4928], wi[:, 3840:3904],
             jnp.zeros((D_MODEL, 64), F32)]
    wa = jnp.concatenate(parts, axis=1).astype(BF16)
    wqup = w_q_up[l].reshape(MLA_Q_RANK, MLA_HEADS, MLA_NOPE + MLA_ROPE)
    wqup = jnp.pad(wqup, ((0, 0), (0, 0), (0, MLA_QK_PAD - MLA_NOPE - MLA_ROPE)))
    wqup = wqup.reshape(MLA_Q_RANK, MLA_HEADS * MLA_QK_PAD).astype(BF16)
    wkv = w_kv_up[l].reshape(MLA_KV_RANK, MLA_HEADS, MLA_NOPE + MLA_V)
    wkn = wkv[:, :, :MLA_NOPE].reshape(MLA_KV_RANK, MLA_HEADS * MLA_NOPE).astype(BF16)
    wv = wkv[:, :, MLA_NOPE:].reshape(MLA_KV_RANK, MLA_HEADS * MLA_V).astype(BF16)
    wgate = wi[:, 4928:].reshape(D_MODEL, N_BRANCH, D_MODEL).transpose(1, 0, 2).astype(BF16)
    return dict(wa=wa, wqup=wqup, wkn=wkn, wv=wv,
                qng=q_norm_g[l][None, :], kvng=kv_norm_g[l][None, :],
                wgate=wgate, wb=w_branch[l].astype(BF16), wo=w_out[l].astype(BF16),
                wl=ssm_glu_w[l][:, :SSM_CH].astype(BF16), wgl=ssm_glu_w[l][:, SSM_CH:].astype(BF16),
                wxq=w_xq[l].astype(BF16), wxkv=w_xkv[l].astype(BF16), wxo=w_xo[l].astype(BF16))


def _forward(x, mem, positions, w_in, w_q_up, w_kv_up, q_norm_g, kv_norm_g, diff_lambda, diff_subln_g,
             ssm_lam_re, ssm_lam_im, ssm_b_re, ssm_b_im, ssm_c_re, ssm_c_im, ssm_d, ssm_log_dt, ssm_glu_w,
             w_branch, w_out, ln_mix_g, ln_mix_b, w_xq, w_xkv, w_xo, ln_x_g, ln_x_b, ffn_w_gate_up,
             ffn_w_down, moe_router, moe_w_gate_up, moe_w_down, ln_ffn_g, ln_ffn_b, *, ssm_tc, attn_tq):
    s = x.shape[0]
    tabs = _rope_tables(positions)
    row = lambda v: v[None, :]
    ones_g = jnp.ones((1, 128), F32)
    dummy_c = jnp.zeros((s, 128), F32)
    for l in range(DEPTH):
        pk = _pack_layer(l, w_in, w_q_up, w_kv_up, q_norm_g, kv_norm_g, w_branch, w_out, ssm_glu_w,
                         w_xq, w_xkv, w_xo)
        qd, kd, vd, qm, km, vm, su = _inproj(x, pk['wa'], pk['wqup'], pk['wkn'], pk['wv'], pk['qng'],
                                             pk['kvng'], tabs)
        lam_init = 0.8 - 0.6 * math.exp(-0.3 * l)
        lf = diff_lambda[l]
        lam = (jnp.exp(jnp.sum(lf[0] * lf[1])) - jnp.exp(jnp.sum(lf[2] * lf[3])) + lam_init).reshape(1)
        o_diff = _attention(qd, kd, vd, lam, row(diff_subln_g[l]), heads=DIFF_HEADS, dqk=128, diff=True,
                            post_scale=1.0 - lam_init, tq=min(2 * attn_tq, s), tk=attn_tq)
        o_mla = _attention(qm, km, vm, lam, ones_g, heads=MLA_HEADS, dqk=MLA_QK_PAD, diff=False,
                           tq=min(4 * attn_tq, s), tk=attn_tq)
        sp = _ssm_params(ssm_lam_re[l], ssm_lam_im[l], ssm_b_re[l], ssm_b_im[l], ssm_c_re[l], ssm_c_im[l],
                         ssm_log_dt[l], s // SSM_SEGS)
        y = _ssm(su, sp, row(ssm_d[l]), ssm_tc)
        o_ssm = _glu(y, pk['wl'], pk['wgl'])
        x = _merge(x, o_diff, o_mla, o_ssm, pk['wgate'], pk['wb'], pk['wo'], row(ln_mix_g[l]), row(ln_mix_b[l]))
        kv = _memkv(mem, pk['wxkv'])
        x = _xattn(x, pk['wxq'], kv, pk['wxo'], row(ln_x_g[l]), row(ln_x_b[l]))
        if l % 2 == 0:
            x = _ffn(x, dummy_c, ffn_w_gate_up[l // 2][None].astype(BF16), ffn_w_down[l // 2][None].astype(BF16),
                     row(ln_ffn_g[l]), row(ln_ffn_b[l]), weighted=False)
        else:
            wr = jnp.pad(moe_router[l // 2], ((0, 0), (0, 128 - N_EXPERTS)))
            combine = _router(x, wr)
            x = _ffn(x, combine, moe_w_gate_up[l // 2].astype(BF16), moe_w_down[l // 2].astype(BF16),
                     row(ln_ffn_g[l]), row(ln_ffn_b[l]), weighted=True)
    return x


def kernel(x, mem, positions, w_in, w_q_up, w_kv_up, q_norm_g, kv_norm_g, diff_lambda, diff_subln_g, ssm_lam_re, ssm_lam_im, ssm_b_re, ssm_b_im, ssm_c_re, ssm_c_im, ssm_d, ssm_log_dt, ssm_glu_w, w_branch, w_out, ln_mix_g, ln_mix_b, w_xq, w_xkv, w_xo, ln_x_g, ln_x_b, ffn_w_gate_up, ffn_w_down, moe_router, moe_w_gate_up, moe_w_down, ln_ffn_g, ln_ffn_b):
    assert x.shape[0] == 1 and mem.shape[0] == 1
    seq = x.shape[1]
    out = _forward(x[0], mem[0], positions[0], w_in, w_q_up, w_kv_up, q_norm_g, kv_norm_g, diff_lambda,
                   diff_subln_g, ssm_lam_re, ssm_lam_im, ssm_b_re, ssm_b_im, ssm_c_re, ssm_c_im, ssm_d,
                   ssm_log_dt, ssm_glu_w, w_branch, w_out, ln_mix_g, ln_mix_b, w_xq, w_xkv, w_xo, ln_x_g,
                   ln_x_b, ffn_w_gate_up, ffn_w_down, moe_router, moe_w_gate_up, moe_w_down, ln_ffn_g,
                   ln_ffn_b, ssm_tc=min(256, seq // SSM_SEGS), attn_tq=min(512, seq))
    return out[None]
```

```python
import functools
import math

import jax
import jax.numpy as jnp
from jax import lax
from jax.experimental import pallas as pl
from jax.experimental.pallas import tpu as pltpu

D_MODEL = 2048
DEPTH = 2
MEM_LEN = 256
DIFF_HEADS = 8
DIFF_HEAD_DIM = 64
DIFF_V_DIM = 128
DIFF_ROT_DIM = 16
DIFF_QK_W = 1024
DIFF_V_W = 1024
MLA_HEADS = 8
MLA_Q_RANK = 512
MLA_KV_RANK = 256
MLA_NOPE = 128
MLA_ROPE = 64
MLA_V = 128
MLA_QK_PAD = 256
V_ROWS = 144
SSM_CH = 1024
SSM_GROUP = 16
SSM_GROUPS = 64
SSM_STATE = 64
SSM_SEGS = 8
SSM_BLOCKS = 8
SSM_BLOCK_STATE = 512
N_BRANCH = 3
BRANCH_W = 1024
X_HEADS = 4
X_HEAD_DIM = 128
X_W = 512
D_FF = 5632
N_EXPERTS = 8
ROPE_THETA = 500000.0
LN_EPS = 1e-5
RMS_EPS = 1e-6
NEG_BIG = -1e30
ALPHA = (2 * DEPTH) ** 0.25
LOG2E = math.log2(math.e)

A_DQ, A_DK, A_DV, A_MQ, A_MKV, A_SU, A_MKR, A_END = 0, 1024, 2048, 3072, 3584, 3840, 4864, 4992

VMEM_LIMIT = 56 * 1024 * 1024
F32 = jnp.float32
BF16 = jnp.bfloat16
HI = lax.Precision.HIGHEST


def _cparams(sem):
    return pltpu.CompilerParams(dimension_semantics=sem, vmem_limit_bytes=VMEM_LIMIT)


def _layer_norm(z, g, b):
    mu = jnp.mean(z, axis=-1, keepdims=True)
    zc = z - mu
    var = jnp.mean(zc * zc, axis=-1, keepdims=True)
    return zc * lax.rsqrt(var + LN_EPS) * g + b


def _rms_norm(z, g):
    return z * lax.rsqrt(jnp.mean(z * z, axis=-1, keepdims=True) + RMS_EPS) * g


def _dot(a, b):
    return jnp.dot(a, b, preferred_element_type=F32)


def _dot_nt(a, b):
    return lax.dot_general(a, b, (((1,), (1,)), ((), ())), preferred_element_type=F32)


def _inproj_kernel(x_ref, wa_ref, wqup_ref, wkn_ref, wv_ref, qng_ref, kvng_ref,
                   cd_ref, sda_ref, sdb_ref, cm_ref, sma_ref, smb_ref,
                   qd_ref, kd_ref, vd_ref, qm_ref, km_ref, vm_ref, su_ref):
    xb = x_ref[...].astype(BF16)
    h = _dot(xb, wa_ref[...])
    cd, sda, sdb = cd_ref[...], sda_ref[...], sdb_ref[...]
    cm, sma, smb = cm_ref[...], sma_ref[...], smb_ref[...]

    def rope_d(blk):
        return blk * cd + pltpu.roll(blk, 8, 1) * sda + pltpu.roll(blk, 120, 1) * sdb

    def rope_m(blk):
        return blk * cm + pltpu.roll(blk, 32, 1) * sma + pltpu.roll(blk, 96, 1) * smb

    q_scale = DIFF_HEAD_DIM ** -0.5 * LOG2E
    for c in range(DIFF_QK_W // 128):
        lo, hi = c * 128, (c + 1) * 128
        qd_ref[:, lo:hi] = (rope_d(h[:, A_DQ + lo:A_DQ + hi]) * q_scale).astype(BF16)
        kd_ref[:, lo:hi] = rope_d(h[:, A_DK + lo:A_DK + hi]).astype(BF16)
    vd_ref[...] = h[:, A_DV:A_MQ].astype(BF16)
    su_ref[...] = h[:, A_SU:A_MKR]

    cq = _rms_norm(h[:, A_MQ:A_MKV], qng_ref[...]).astype(BF16)
    qm = _dot(cq, wqup_ref[...]) * ((MLA_NOPE + MLA_ROPE) ** -0.5 * LOG2E)
    ckv = _rms_norm(h[:, A_MKV:A_SU], kvng_ref[...]).astype(BF16)
    kn = _dot(ckv, wkn_ref[...])
    vm_ref[...] = _dot(ckv, wv_ref[...]).astype(BF16)
    krot = rope_m(h[:, A_MKR:A_END]).astype(BF16)
    for hh in range(MLA_HEADS):
        b0 = hh * MLA_QK_PAD
        qm_ref[:, b0:b0 + 128] = qm[:, b0:b0 + 128].astype(BF16)
        qm_ref[:, b0 + 128:b0 + 256] = rope_m(qm[:, b0 + 128:b0 + 256]).astype(BF16)
        km_ref[:, b0:b0 + 128] = kn[:, hh * 128:(hh + 1) * 128].astype(BF16)
        km_ref[:, b0 + 128:b0 + 256] = krot


def _inproj(x, wa, wqup, wkn, wv, qng, kvng, tabs, tm=256):
    s = x.shape[0]
    row = lambda w: pl.BlockSpec((tm, w), lambda i: (i, 0))
    const = lambda a: pl.BlockSpec(a.shape, lambda i: (0,) * a.ndim, pipeline_mode=pl.Buffered(1))
    out_shapes = [jax.ShapeDtypeStruct((s, 1024), BF16), jax.ShapeDtypeStruct((s, 1024), BF16),
                  jax.ShapeDtypeStruct((s, 1024), BF16), jax.ShapeDtypeStruct((s, 2048), BF16),
                  jax.ShapeDtypeStruct((s, 2048), BF16), jax.ShapeDtypeStruct((s, 1024), BF16),
                  jax.ShapeDtypeStruct((s, 1024), F32)]
    return pl.pallas_call(
        _inproj_kernel,
        grid=(s // tm,),
        in_specs=[row(D_MODEL), const(wa), const(wqup), const(wkn), const(wv), const(qng), const(kvng)]
                 + [row(128)] * 6,
        out_specs=[row(1024), row(1024), row(1024), row(2048), row(2048), row(1024), row(1024)],
        out_shape=out_shapes,
        compiler_params=_cparams(("parallel",)),
        name="inproj",
    )(x, wa, wqup, wkn, wv, qng, kvng, *tabs)


def _col_reduce(op, s, ways=8):
    parts = s.reshape(ways, s.shape[0] // ways, s.shape[1])
    return op(op(parts, axis=0), axis=0, keepdims=True)


def _attn_kernel(lam_ref, qt_ref, k_ref, vt_ref, g_ref, o_ref, q2_sc, m_sc, acc_sc, *,
                 tq, tk, cw, diff, post_scale):
    qi = pl.program_id(1)
    if diff:
        qt = qt_ref[...]
        feat = lax.broadcasted_iota(jnp.int32, qt.shape, 0)
        zero = jnp.zeros_like(qt)
        q2_sc[:, :tq] = jnp.where(feat < DIFF_HEAD_DIM, qt, zero)
        q2_sc[:, tq:] = jnp.where(feat >= DIFF_HEAD_DIM, qt, zero)
    else:
        q2_sc[...] = qt_ref[...]
    m_sc[...] = jnp.full(m_sc.shape, NEG_BIG, F32)
    acc_sc[...] = jnp.zeros(acc_sc.shape, F32)
    ratio = tq // tk
    cols = q2_sc.shape[1]

    def step(j, diag):
        off = pl.multiple_of(j * tk, tk)
        kj = k_ref[pl.ds(off, tk), :]
        vj = vt_ref[j]
        q0 = lambda c: (c * cw) % tq
        blocks = [c for c in range(cols // cw) if diag is None or diag * tk < q0(c) + cw]
        qk = lambda c: _dot(kj, q2_sc[:, c * cw:(c + 1) * cw])
        s_next = qk(blocks[0])
        for n, c in enumerate(blocks):
            s = s_next
            if n + 1 < len(blocks):
                s_next = qk(blocks[n + 1])
            cs = slice(c * cw, (c + 1) * cw)
            if diag is not None:
                kpos = lax.broadcasted_iota(jnp.int32, s.shape, 0) + diag * tk
                qpos = lax.broadcasted_iota(jnp.int32, s.shape, 1) + q0(c)
                s = jnp.where(qpos >= kpos, s, NEG_BIG)
            m_old = m_sc[:, cs]
            m_new = jnp.maximum(m_old, _col_reduce(jnp.max, s))
            a = jnp.exp2(m_old - m_new)
            p = jnp.exp2((s - m_new).astype(BF16))
            acc_sc[:, cs] = a * acc_sc[:, cs] + _dot(vj, p)
            m_sc[:, cs] = m_new

    def body(j, carry):
        step(j, None)
        return carry

    lax.fori_loop(0, qi * ratio, body, 0)
    for d in range(ratio):
        step(qi * ratio + d, d)
    ot = acc_sc[:128, :] / acc_sc[128:129, :]
    if diff:
        ot = ot[:, :tq] - lam_ref[0] * ot[:, tq:]
    o = ot.T
    if diff:
        o = _rms_norm(o, g_ref[...]) * post_scale
    o_ref[...] = o.astype(o_ref.dtype)


def _attention(q, k, v, lam, g, *, heads, dqk, diff, post_scale=1.0, tq=512, tk=512, cw=256):
    s = q.shape[0]
    cols = 2 * tq if diff else tq
    nch = s // tk
    qt = q.T
    vt = v.reshape(nch, tk, heads, 128).transpose(2, 0, 3, 1)
    ones = jnp.ones((heads, nch, V_ROWS - 128, tk), BF16)
    vt = jnp.concatenate([vt, ones], axis=2).reshape(heads * nch, V_ROWS, tk)
    kern = functools.partial(_attn_kernel, tq=tq, tk=tk, cw=cw, diff=diff, post_scale=post_scale)
    return pl.pallas_call(
        kern,
        grid=(heads, s // tq),
        in_specs=[pl.BlockSpec(memory_space=pltpu.SMEM),
                  pl.BlockSpec((dqk, tq), lambda h, i: (h, i)),
                  pl.BlockSpec((s, dqk), lambda h, i: (0, h)),
                  pl.BlockSpec((nch, V_ROWS, tk), lambda h, i: (h, 0, 0)),
                  pl.BlockSpec((1, 128), lambda h, i: (0, 0))],
        out_specs=pl.BlockSpec((tq, 128), lambda h, i: (i, h)),
        out_shape=jax.ShapeDtypeStruct((s, heads * 128), BF16),
        scratch_shapes=[pltpu.VMEM((dqk, cols), BF16), pltpu.VMEM((1, cols), F32),
                        pltpu.VMEM((V_ROWS, cols), F32)],
        compiler_params=_cparams(("parallel", "parallel")),
        name="diff_attn" if diff else "mla_attn",
    )(lam, qt, k, vt, g)


def _ssm_kernel(u_ref, wb_ref, ar_ref, ai_ref, wc_ref, d_ref, x0_ref, *refs, tc, emit_y):
    if emit_y:
        y_ref, x_sc, s_sc = refs
    else:
        xe_ref, x_sc, s_sc = refs
    i = pl.program_id(1)
    ns = SSM_BLOCK_STATE

    @pl.when(i == 0)
    def _():
        s_sc[...] = x0_ref[0]

    u = u_ref[...]
    x_sc[...] = _dot(u.astype(BF16), wb_ref[0])
    ar = jnp.broadcast_to(ar_ref[0], (SSM_SEGS, ns))
    ai = jnp.broadcast_to(ai_ref[0], (SSM_SEGS, ns))

    def body(t, carry):
        re, im = carry
        rows = pl.ds(pl.multiple_of(t * SSM_SEGS, SSM_SEGS), SSM_SEGS)
        nre = ar * re - ai * im + x_sc[rows, :ns]
        nim = ar * im + ai * re + x_sc[rows, ns:]
        x_sc[rows, :ns] = nre
        x_sc[rows, ns:] = nim
        return nre, nim

    re, im = lax.fori_loop(0, tc, body, (s_sc[:, :ns], s_sc[:, ns:]), unroll=8)
    s_sc[:, :ns] = re
    s_sc[:, ns:] = im
    if emit_y:
        y = _dot(x_sc[...].astype(BF16), wc_ref[0]) + d_ref[...] * u
        y_ref[...] = jax.nn.gelu(y).astype(y_ref.dtype)
    else:
        @pl.when(i == pl.num_programs(1) - 1)
        def _():
            xe_ref[0] = s_sc[...]


def _ssm_pass(u_perm, p, d, x0, tc, emit_y):
    rows = u_perm.shape[0]
    nchunk = rows // (tc * SSM_SEGS)
    blk3 = lambda a: pl.BlockSpec((1,) + a.shape[1:], lambda j, i: (j, 0, 0))
    if emit_y:
        out_specs = pl.BlockSpec((tc * SSM_SEGS, 128), lambda j, i: (i, j))
        out_shape = jax.ShapeDtypeStruct((rows, SSM_CH), BF16)
    else:
        out_specs = blk3(x0)
        out_shape = jax.ShapeDtypeStruct(x0.shape, F32)
    return pl.pallas_call(
        functools.partial(_ssm_kernel, tc=tc, emit_y=emit_y),
        grid=(SSM_BLOCKS, nchunk),
        in_specs=[pl.BlockSpec((tc * SSM_SEGS, 128), lambda j, i: (i, j)),
                  blk3(p['wb']), blk3(p['ar']), blk3(p['ai']), blk3(p['wc']),
                  pl.BlockSpec((1, 128), lambda j, i: (0, j)), blk3(x0)],
        out_specs=out_specs,
        out_shape=out_shape,
        scratch_shapes=[pltpu.VMEM((tc * SSM_SEGS, 2 * SSM_BLOCK_STATE), F32),
                        pltpu.VMEM((SSM_SEGS, 2 * SSM_BLOCK_STATE), F32)],
        compiler_params=_cparams(("parallel", "arbitrary")),
        name="ssm_scan" if emit_y else "ssm_state",
    )(u_perm, p['wb'], p['ar'], p['ai'], p['wc'], d, x0)


def _ssm_params(lam_re, lam_im, b_re, b_im, c_re, c_im, log_dt, seg_len):
    dt = jnp.exp(log_dt)[:, None]

    def cexp(scale):
        mag = jnp.exp(lam_re * scale)
        return mag * jnp.cos(lam_im * scale), mag * jnp.sin(lam_im * scale)

    ar, ai = cexp(dt)
    sr, si = cexp(dt * seg_len)
    den = lam_re * lam_re + lam_im * lam_im
    fr = ((ar - 1.0) * lam_re + ai * lam_im) / den
    fi = (ai * lam_re - (ar - 1.0) * lam_im) / den
    bbr = fr[..., None] * b_re - fi[..., None] * b_im
    bbi = fr[..., None] * b_im + fi[..., None] * b_re
    eye = jnp.eye(SSM_BLOCKS, dtype=F32)

    def in_mat(w):
        w = w.reshape(SSM_BLOCKS, 8, SSM_STATE, SSM_GROUP)
        return jnp.einsum('jgnp,gh->jgphn', w, eye).reshape(SSM_BLOCKS, 8 * SSM_GROUP, 8 * SSM_STATE)

    def out_mat(w):
        w = w.reshape(SSM_BLOCKS, 8, SSM_GROUP, SSM_STATE)
        return jnp.einsum('jgpn,gh->jgnhp', w, eye).reshape(SSM_BLOCKS, 8 * SSM_STATE, 8 * SSM_GROUP)

    vec = lambda a: a.reshape(SSM_BLOCKS, 1, SSM_BLOCK_STATE)
    return dict(wb=jnp.concatenate([in_mat(bbr), in_mat(bbi)], axis=2).astype(BF16),
                wc=jnp.concatenate([out_mat(c_re), -out_mat(c_im)], axis=1).astype(BF16),
                ar=vec(ar), ai=vec(ai), sr=vec(sr), si=vec(si))


def _ssm(su, p, d, tc):
    s = su.shape[0]
    seg_len = s // SSM_SEGS
    u_perm = su.reshape(SSM_SEGS, seg_len, SSM_CH).transpose(1, 0, 2).reshape(s, SSM_CH)
    ns = SSM_BLOCK_STATE
    zeros = jnp.zeros((SSM_BLOCKS, SSM_SEGS, 2 * ns), F32)
    xe = _ssm_pass(u_perm, p, d, zeros, tc, emit_y=False)
    x0r, x0i = [zeros[:, 0, :ns]], [zeros[:, 0, ns:]]
    for sg in range(SSM_SEGS - 1):
        pr, pi = x0r[-1], x0i[-1]
        x0r.append(p['sr'][:, 0] * pr - p['si'][:, 0] * pi + xe[:, sg, :ns])
        x0i.append(p['sr'][:, 0] * pi + p['si'][:, 0] * pr + xe[:, sg, ns:])
    x0 = jnp.concatenate([jnp.stack(x0r, axis=1), jnp.stack(x0i, axis=1)], axis=2)
    y = _ssm_pass(u_perm, p, d, x0, tc, emit_y=True)
    return y.reshape(seg_len, SSM_SEGS, SSM_CH).transpose(1, 0, 2).reshape(s, SSM_CH)


def _glu_kernel(y_ref, wl_ref, wg_ref, o_ref):
    y = y_ref[...]
    o_ref[...] = (_dot(y, wl_ref[...]) * jax.nn.sigmoid(_dot(y, wg_ref[...]))).astype(o_ref.dtype)


def _glu(y, wl, wg, tm=512):
    s = y.shape[0]
    return pl.pallas_call(
        _glu_kernel,
        grid=(s // tm,),
        in_specs=[pl.BlockSpec((tm, SSM_CH), lambda i: (i, 0)),
                  pl.BlockSpec(wl.shape, lambda i: (0, 0)),
                  pl.BlockSpec(wg.shape, lambda i: (0, 0))],
        out_specs=pl.BlockSpec((tm, SSM_CH), lambda i: (i, 0)),
        out_shape=jax.ShapeDtypeStruct((s, SSM_CH), BF16),
        compiler_params=_cparams(("parallel",)),
        name="ssm_glu",
    )(y, wl, wg)


def _merge_kernel(x_ref, od_ref, om_ref, os_ref, wg_ref, wb_ref, wo_ref, g_ref, b_ref, o_ref, xb_sc):
    j = pl.program_id(1)

    @pl.when(j == 0)
    def _():
        xb_sc[...] = x_ref[...].astype(BF16)
        o_ref[...] = jnp.zeros(o_ref.shape, F32)

    xb = xb_sc[...]
    merged = None
    for n, br_ref in enumerate((od_ref, om_ref, os_ref)):
        term = jax.nn.sigmoid(_dot(xb, wg_ref[n])) * _dot(br_ref[...], wb_ref[n])
        merged = term if merged is None else merged + term
    o_ref[...] += _dot(merged.astype(BF16), wo_ref[...])

    @pl.when(j == pl.num_programs(1) - 1)
    def _():
        o_ref[...] = _layer_norm(ALPHA * x_ref[...] + o_ref[...], g_ref[...], b_ref[...])


def _merge(x, od, om, os_, wg, wb, wo, g, b, tm=512, tn=256):
    s = x.shape[0]
    br = pl.BlockSpec((tm, BRANCH_W), lambda i, j: (i, 0))
    vec = pl.BlockSpec((1, D_MODEL), lambda i, j: (0, 0))
    return pl.pallas_call(
        _merge_kernel,
        grid=(s // tm, D_MODEL // tn),
        in_specs=[pl.BlockSpec((tm, D_MODEL), lambda i, j: (i, 0)), br, br, br,
                  pl.BlockSpec((N_BRANCH, D_MODEL, tn), lambda i, j: (0, 0, j)),
                  pl.BlockSpec((N_BRANCH, BRANCH_W, tn), lambda i, j: (0, 0, j)),
                  pl.BlockSpec((tn, D_MODEL), lambda i, j: (j, 0)), vec, vec],
        out_specs=pl.BlockSpec((tm, D_MODEL), lambda i, j: (i, 0)),
        out_shape=jax.ShapeDtypeStruct((s, D_MODEL), F32),
        scratch_shapes=[pltpu.VMEM((tm, D_MODEL), BF16)],
        compiler_params=_cparams(("parallel", "arbitrary")),
        name="merge_out_ln",
    )(x, od, om, os_, wg, wb, wo, g, b)


def _memkv_kernel(m_ref, w_ref, o_ref):
    o_ref[...] = _dot(m_ref[...].astype(BF16), w_ref[...]).astype(o_ref.dtype)


def _memkv(mem, w):
    return pl.pallas_call(
        _memkv_kernel,
        out_shape=jax.ShapeDtypeStruct((mem.shape[0], w.shape[1]), BF16),
        compiler_params=pltpu.CompilerParams(vmem_limit_bytes=VMEM_LIMIT),
        name="mem_kv",
    )(mem, w)


def _xattn_kernel(x_ref, wq_ref, kv_ref, wo_ref, g_ref, b_ref, o_ref):
    x = x_ref[...]
    q = (_dot(x.astype(BF16), wq_ref[...]) * (X_HEAD_DIM ** -0.5)).astype(BF16)
    outs = []
    for h in range(X_HEADS):
        lo, hi = h * X_HEAD_DIM, (h + 1) * X_HEAD_DIM
        s = _dot_nt(q[:, lo:hi], kv_ref[:, lo:hi])
        p = jnp.exp(s - jnp.max(s, axis=1, keepdims=True))
        o = _dot(p.astype(BF16), kv_ref[:, X_W + lo:X_W + hi])
        outs.append((o / jnp.sum(p, axis=1, keepdims=True)).astype(BF16))
    xa = _dot(jnp.concatenate(outs, axis=1), wo_ref[...])
    o_ref[...] = _layer_norm(ALPHA * x + xa, g_ref[...], b_ref[...])


def _xattn(x, wq, kv, wo, g, b, tm=512):
    s = x.shape[0]
    full = lambda a: pl.BlockSpec(a.shape, lambda i: (0, 0))
    return pl.pallas_call(
        _xattn_kernel,
        grid=(s // tm,),
        in_specs=[pl.BlockSpec((tm, D_MODEL), lambda i: (i, 0)), full(wq), full(kv), full(wo), full(g), full(b)],
        out_specs=pl.BlockSpec((tm, D_MODEL), lambda i: (i, 0)),
        out_shape=jax.ShapeDtypeStruct((s, D_MODEL), F32),
        compiler_params=_cparams(("parallel",)),
        name="xattn_ln",
    )(x, wq, kv, wo, g, b)


def _ffn_kernel(x_ref, wg_ref, wu_ref, wd_ref, g_ref, b_ref, o_ref, xb_sc):
    f = pl.program_id(1)

    @pl.when(f == 0)
    def _():
        xb_sc[...] = x_ref[...].astype(BF16)
        o_ref[...] = jnp.zeros(o_ref.shape, F32)

    xb = xb_sc[...]
    a = jax.nn.silu(_dot(xb, wg_ref[...])) * _dot(xb, wu_ref[...])
    o_ref[...] += _dot(a.astype(BF16), wd_ref[...])

    @pl.when(f == pl.num_programs(1) - 1)
    def _():
        o_ref[...] = _layer_norm(ALPHA * x_ref[...] + o_ref[...], g_ref[...], b_ref[...])


def _ffn(x, wgu, wd, g, b, tm=512, tf=512):
    s = x.shape[0]
    nf = D_FF // tf
    vec = pl.BlockSpec((1, D_MODEL), lambda i, f: (0, 0))
    return pl.pallas_call(
        _ffn_kernel,
        grid=(s // tm, nf),
        in_specs=[pl.BlockSpec((tm, D_MODEL), lambda i, f: (i, 0)),
                  pl.BlockSpec((D_MODEL, tf), lambda i, f: (0, f)),
                  pl.BlockSpec((D_MODEL, tf), lambda i, f: (0, nf + f)),
                  pl.BlockSpec((tf, D_MODEL), lambda i, f: (f, 0)), vec, vec],
        out_specs=pl.BlockSpec((tm, D_MODEL), lambda i, f: (i, 0)),
        out_shape=jax.ShapeDtypeStruct((s, D_MODEL), F32),
        scratch_shapes=[pltpu.VMEM((tm, D_MODEL), BF16)],
        compiler_params=_cparams(("parallel", "arbitrary")),
        name="ffn_ln",
    )(x, wgu, wgu, wd, g, b)


def _router_kernel(x_ref, w_ref, wt_ref, ix_ref):
    logits = jnp.dot(x_ref[...], w_ref[...], precision=HI, preferred_element_type=F32)
    lane = lax.broadcasted_iota(jnp.int32, logits.shape, 1)
    logits = jnp.where(lane < N_EXPERTS, logits, NEG_BIG)
    v1 = jnp.max(logits, axis=1, keepdims=True)
    i1 = jnp.min(jnp.where(logits == v1, lane, 128), axis=1, keepdims=True)
    rest = jnp.where(lane == i1, NEG_BIG, logits)
    v2 = jnp.max(rest, axis=1, keepdims=True)
    i2 = jnp.min(jnp.where(rest == v2, lane, 128), axis=1, keepdims=True)
    e2 = jnp.exp(v2 - v1)
    w1 = 1.0 / (1.0 + e2)
    w2 = e2 / (1.0 + e2)
    wt_ref[...] = jnp.where(lane == 0, w1, jnp.where(lane == 1, w2, 0.0))
    ix_ref[...] = jnp.where(lane == 0, i1, jnp.where(lane == 1, i2, 0))


def _router(x, w_pad, tm=512):
    s = x.shape[0]
    out = pl.BlockSpec((tm, 128), lambda i: (i, 0))
    return pl.pallas_call(
        _router_kernel,
        grid=(s // tm,),
        in_specs=[pl.BlockSpec((tm, D_MODEL), lambda i: (i, 0)), pl.BlockSpec(w_pad.shape, lambda i: (0, 0))],
        out_specs=[out, out],
        out_shape=[jax.ShapeDtypeStruct((s, 128), F32), jax.ShapeDtypeStruct((s, 128), jnp.int32)],
        compiler_params=_cparams(("parallel",)),
        name="router_top2",
    )(x, w_pad)


def _moe_plan(expert_idx, tm):
    n_pairs = expert_idx.shape[0] * 2
    n_tiles = n_pairs // tm + N_EXPERTS
    ef = expert_idx.reshape(-1)
    onehot = (ef[:, None] == jnp.arange(N_EXPERTS)[None, :]).astype(jnp.int32)
    rank = jnp.take_along_axis(jnp.cumsum(onehot, axis=0) - onehot, ef[:, None], axis=1)[:, 0]
    padded = (jnp.sum(onehot, axis=0) + tm - 1) // tm * tm
    ends = jnp.cumsum(padded)
    dest = (ends - padded)[ef] + rank
    src = jnp.zeros((n_tiles * tm,), jnp.int32).at[dest].set(jnp.arange(n_pairs, dtype=jnp.int32) // 2)
    n_valid = ends[-1] // tm
    tile_start = jnp.minimum(jnp.arange(n_tiles), n_valid - 1) * tm
    tile_expert = jnp.sum((tile_start[:, None] >= ends[None, :]).astype(jnp.int32), axis=1)
    return src.reshape(n_tiles, 1, tm), dest, tile_expert.astype(jnp.int32), n_valid.reshape(1).astype(jnp.int32)


def _gather_rows(src_hbm, row_of, dst_vmem, sem, n):
    def copy(r, row):
        return pltpu.make_async_copy(src_hbm.at[pl.ds(row, 1), :], dst_vmem.at[pl.ds(r, 1), :], sem)

    def start(r, carry):
        copy(r, row_of(r)).start()
        return carry

    def wait(r, carry):
        copy(r, 0).wait()
        return carry

    lax.fori_loop(0, n, start, 0, unroll=8)
    lax.fori_loop(0, n, wait, 0, unroll=8)


def _moe_ffn_kernel(te_ref, nv_ref, src_ref, x_hbm, wg_ref, wu_ref, wd_ref, o_ref, xg_sc, xb_sc, sem, *, tm):
    i, f = pl.program_id(0), pl.program_id(1)
    valid = i < nv_ref[0]

    @pl.when(f == 0)
    def _():
        o_ref[...] = jnp.zeros(o_ref.shape, F32)

    @pl.when(valid & (f == 0))
    def _():
        _gather_rows(x_hbm, lambda r: src_ref[0, 0, r], xg_sc, sem, tm)
        xb_sc[...] = xg_sc[...].astype(BF16)

    @pl.when(valid)
    def _():
        xb = xb_sc[...]
        a = jax.nn.silu(_dot(xb, wg_ref[0])) * _dot(xb, wu_ref[0])
        o_ref[...] += _dot(a.astype(BF16), wd_ref[0])


def _moe_ffn(x, src, tile_expert, n_valid, wgu, wd, tm, tf=512):
    n_tiles = src.shape[0]
    nf = D_FF // tf
    fcol = lambda i, f, nv: jnp.where(i < nv[0], f, nf - 1)
    grid_spec = pltpu.PrefetchScalarGridSpec(
        num_scalar_prefetch=2,
        grid=(n_tiles, nf),
        in_specs=[pl.BlockSpec((1, 1, tm), lambda i, f, te, nv: (i, 0, 0), memory_space=pltpu.SMEM),
                  pl.BlockSpec(memory_space=pl.ANY),
                  pl.BlockSpec((1, D_MODEL, tf), lambda i, f, te, nv: (te[i], 0, fcol(i, f, nv))),
                  pl.BlockSpec((1, D_MODEL, tf), lambda i, f, te, nv: (te[i], 0, nf + fcol(i, f, nv))),
                  pl.BlockSpec((1, tf, D_MODEL), lambda i, f, te, nv: (te[i], fcol(i, f, nv), 0))],
        out_specs=pl.BlockSpec((tm, D_MODEL), lambda i, f, te, nv: (i, 0)),
        scratch_shapes=[pltpu.VMEM((tm, D_MODEL), F32), pltpu.VMEM((tm, D_MODEL), BF16),
                        pltpu.SemaphoreType.DMA(())])
    return pl.pallas_call(
        functools.partial(_moe_ffn_kernel, tm=tm),
        grid_spec=grid_spec,
        out_shape=jax.ShapeDtypeStruct((n_tiles * tm, D_MODEL), F32),
        compiler_params=_cparams(("arbitrary", "arbitrary")),
        name="moe_ffn",
    )(tile_expert, n_valid, src, x, wgu, wgu, wd)


def _moe_combine_kernel(p1_ref, p2_ref, x_ref, wt_ref, y_hbm, g_ref, b_ref, o_ref, y1_sc, y2_sc, sem, *, tm):
    _gather_rows(y_hbm, lambda r: p1_ref[0, 0, r], y1_sc, sem, tm)
    _gather_rows(y_hbm, lambda r: p2_ref[0, 0, r], y2_sc, sem, tm)
    wt = wt_ref[...]
    f = wt[:, 0:1] * y1_sc[...] + wt[:, 1:2] * y2_sc[...]
    o_ref[...] = _layer_norm(ALPHA * x_ref[...] + f, g_ref[...], b_ref[...])


def _moe_combine(x, wts, y_sorted, dest, g, b, tm=256):
    s = x.shape[0]
    pos = dest.reshape(s // tm, tm, 2)
    p1, p2 = pos[:, None, :, 0], pos[:, None, :, 1]
    smem = pl.BlockSpec((1, 1, tm), lambda i: (i, 0, 0), memory_space=pltpu.SMEM)
    vec = pl.BlockSpec((1, D_MODEL), lambda i: (0, 0))
    return pl.pallas_call(
        functools.partial(_moe_combine_kernel, tm=tm),
        grid=(s // tm,),
        in_specs=[smem, smem, pl.BlockSpec((tm, D_MODEL), lambda i: (i, 0)),
                  pl.BlockSpec((tm, 128), lambda i: (i, 0)), pl.BlockSpec(memory_space=pl.ANY), vec, vec],
        out_specs=pl.BlockSpec((tm, D_MODEL), lambda i: (i, 0)),
        out_shape=jax.ShapeDtypeStruct((s, D_MODEL), F32),
        scratch_shapes=[pltpu.VMEM((tm, D_MODEL), F32), pltpu.VMEM((tm, D_MODEL), F32),
                        pltpu.SemaphoreType.DMA(())],
        compiler_params=_cparams(("arbitrary",)),
        name="moe_combine_ln",
    )(p1, p2, x, wts, y_sorted, g, b)


def _rope_tables(positions):
    pos = positions.astype(F32)[:, None]
    lane = jnp.arange(128) % 64

    def tables(rot):
        half = rot // 2
        inv_freq = jnp.power(jnp.float32(ROPE_THETA), -jnp.arange(0, rot, 2, dtype=F32) / rot)
        ang = pos * inv_freq[None, :]
        cos = jnp.take(jnp.cos(ang), lane % half, axis=1)
        sin = jnp.take(jnp.sin(ang), lane % half, axis=1)
        first, second = (lane < half)[None, :], ((lane >= half) & (lane < rot))[None, :]
        c = jnp.where(first | second, cos, 1.0)
        s_prev = jnp.where(second, sin, 0.0)
        s_next = jnp.where(first, -sin, 0.0)
        return c, s_prev, s_next

    return tables(DIFF_ROT_DIM) + tables(MLA_ROPE)


def _pack_layer(l, w_in, w_q_up, w_kv_up, q_norm_g, kv_norm_g, w_branch, w_out, ssm_glu_w, w_xq, w_xkv, w_xo):
    wi = w_in[l]
    parts = [wi[:, 0:3072], wi[:, 3072:3584], wi[:, 3584:3840], wi[:, 3904:4928], wi[:, 3840:3904],
             jnp.zeros((D_MODEL, 64), F32)]
    wa = jnp.concatenate(parts, axis=1).astype(BF16)
    wqup = w_q_up[l].reshape(MLA_Q_RANK, MLA_HEADS, MLA_NOPE + MLA_ROPE)
    wqup = jnp.pad(wqup, ((0, 0), (0, 0), (0, MLA_QK_PAD - MLA_NOPE - MLA_ROPE)))
    wqup = wqup.reshape(MLA_Q_RANK, MLA_HEADS * MLA_QK_PAD).astype(BF16)
    wkv = w_kv_up[l].reshape(MLA_KV_RANK, MLA_HEADS, MLA_NOPE + MLA_V)
    wkn = wkv[:, :, :MLA_NOPE].reshape(MLA_KV_RANK, MLA_HEADS * MLA_NOPE).astype(BF16)
    wv = wkv[:, :, MLA_NOPE:].reshape(MLA_KV_RANK, MLA_HEADS * MLA_V).astype(BF16)
    wgate = wi[:, 4928:].reshape(D_MODEL, N_BRANCH, D_MODEL).transpose(1, 0, 2).astype(BF16)
    return dict(wa=wa, wqup=wqup, wkn=wkn, wv=wv,
                qng=q_norm_g[l][None, :], kvng=kv_norm_g[l][None, :],
                wgate=wgate, wb=w_branch[l].astype(BF16), wo=w_out[l].astype(BF16),
                wl=ssm_glu_w[l][:, :SSM_CH].astype(BF16), wgl=ssm_glu_w[l][:, SSM_CH:].astype(BF16),
                wxq=w_xq[l].astype(BF16), wxkv=w_xkv[l].astype(BF16), wxo=w_xo[l].astype(BF16))


def _forward(x, mem, positions, w_in, w_q_up, w_kv_up, q_norm_g, kv_norm_g, diff_lambda, diff_subln_g,
             ssm_lam_re, ssm_lam_im, ssm_b_re, ssm_b_im, ssm_c_re, ssm_c_im, ssm_d, ssm_log_dt, ssm_glu_w,
             w_branch, w_out, ln_mix_g, ln_mix_b, w_xq, w_xkv, w_xo, ln_x_g, ln_x_b, ffn_w_gate_up,
             ffn_w_down, moe_router, moe_w_gate_up, moe_w_down, ln_ffn_g, ln_ffn_b, *, ssm_tc, attn_tq):
    s = x.shape[0]
    tabs = _rope_tables(positions)
    row = lambda v: v[None, :]
    ones_g = jnp.ones((1, 128), F32)
    moe_tm = min(512, s // 4)
    for l in range(DEPTH):
        pk = _pack_layer(l, w_in, w_q_up, w_kv_up, q_norm_g, kv_norm_g, w_branch, w_out, ssm_glu_w,
                         w_xq, w_xkv, w_xo)
        qd, kd, vd, qm, km, vm, su = _inproj(x, pk['wa'], pk['wqup'], pk['wkn'], pk['wv'], pk['qng'],
                                             pk['kvng'], tabs)
        lam_init = 0.8 - 0.6 * math.exp(-0.3 * l)
        lf = diff_lambda[l]
        lam = (jnp.exp(jnp.sum(lf[0] * lf[1])) - jnp.exp(jnp.sum(lf[2] * lf[3])) + lam_init).reshape(1)
        o_diff = _attention(qd, kd, vd, lam, row(diff_subln_g[l]), heads=DIFF_HEADS, dqk=128, diff=True,
                            post_scale=1.0 - lam_init, tq=min(2 * attn_tq, s), tk=attn_tq)
        o_mla = _attention(qm, km, vm, lam, ones_g, heads=MLA_HEADS, dqk=MLA_QK_PAD, diff=False,
                           tq=min(4 * attn_tq, s), tk=attn_tq)
        sp = _ssm_params(ssm_lam_re[l], ssm_lam_im[l], ssm_b_re[l], ssm_b_im[l], ssm_c_re[l], ssm_c_im[l],
                         ssm_log_dt[l], s // SSM_SEGS)
        y = _ssm(su, sp, row(ssm_d[l]), ssm_tc)
        o_ssm = _glu(y, pk['wl'], pk['wgl'])
        x = _merge(x, o_diff, o_mla, o_ssm, pk['wgate'], pk['wb'], pk['wo'], row(ln_mix_g[l]), row(ln_mix_b[l]))
        kv = _memkv(mem, pk['wxkv'])
        x = _xattn(x, pk['wxq'], kv, pk['wxo'], row(ln_x_g[l]), row(ln_x_b[l]))
        if l % 2 == 0:
            x = _ffn(x, ffn_w_gate_up[l // 2].astype(BF16), ffn_w_down[l // 2].astype(BF16),
                     row(ln_ffn_g[l]), row(ln_ffn_b[l]))
        else:
            wr = jnp.pad(moe_router[l // 2], ((0, 0), (0, 128 - N_EXPERTS)))
            wts, idx = _router(x, wr)
            src, dest, tile_expert, n_valid = _moe_plan(idx[:, :2], moe_tm)
            y_sorted = _moe_ffn(x, src, tile_expert, n_valid, moe_w_gate_up[l // 2].astype(BF16),
                                moe_w_down[l // 2].astype(BF16), moe_tm)
            x = _moe_combine(x, wts, y_sorted, dest, row(ln_ffn_g[l]), row(ln_ffn_b[l]))
    return x


def kernel(x, mem, positions, w_in, w_q_up, w_kv_up, q_norm_g, kv_norm_g, diff_lambda, diff_subln_g, ssm_lam_re, ssm_lam_im, ssm_b_re, ssm_b_im, ssm_c_re, ssm_c_im, ssm_d, ssm_log_dt, ssm_glu_w, w_branch, w_out, ln_mix_g, ln_mix_b, w_xq, w_xkv, w_xo, ln_x_g, ln_x_b, ffn_w_gate_up, ffn_w_down, moe_router, moe_w_gate_up, moe_w_down, ln_ffn_g, ln_ffn_b):
    assert x.shape[0] == 1 and mem.shape[0] == 1
    seq = x.shape[1]
    out = _forward(x[0], mem[0], positions[0], w_in, w_q_up, w_kv_up, q_norm_g, kv_norm_g, diff_lambda,
                   diff_subln_g, ssm_lam_re, ssm_lam_im, ssm_b_re, ssm_b_im, ssm_c_re, ssm_c_im, ssm_d,
                   ssm_log_dt, ssm_glu_w, w_branch, w_out, ln_mix_g, ln_mix_b, w_xq, w_xkv, w_xo, ln_x_g,
                   ln_x_b, ffn_w_gate_up, ffn_w_down, moe_router, moe_w_gate_up, moe_w_down, ln_ffn_g,
                   ln_ffn_b, ssm_tc=min(256, seq // SSM_SEGS), attn_tq=min(512, seq))
    return out[None]
```

```python
import functools
import math

import jax
import jax.numpy as jnp
from jax import lax
from jax.experimental import pallas as pl
from jax.experimental.pallas import tpu as pltpu

D_MODEL = 2048
DEPTH = 2
MEM_LEN = 256
DIFF_HEADS = 8
DIFF_HEAD_DIM = 64
DIFF_V_DIM = 128
DIFF_ROT_DIM = 16
DIFF_QK_W = 1024
DIFF_V_W = 1024
MLA_HEADS = 8
MLA_Q_RANK = 512
MLA_KV_RANK = 256
MLA_NOPE = 128
MLA_ROPE = 64
MLA_V = 128
MLA_QK_PAD = 256
V_ROWS = 144
SSM_CH = 1024
SSM_GROUP = 16
SSM_GROUPS = 64
SSM_STATE = 64
SSM_SEGS = 8
SSM_BLOCKS = 8
SSM_BLOCK_STATE = 512
N_BRANCH = 3
BRANCH_W = 1024
X_HEADS = 4
X_HEAD_DIM = 128
X_W = 512
D_FF = 5632
N_EXPERTS = 8
ROPE_THETA = 500000.0
LN_EPS = 1e-5
RMS_EPS = 1e-6
NEG_BIG = -1e30
ALPHA = (2 * DEPTH) ** 0.25
LOG2E = math.log2(math.e)

A_DQ, A_DK, A_DV, A_MQ, A_MKV, A_SU, A_MKR, A_END = 0, 1024, 2048, 3072, 3584, 3840, 4864, 4992

VMEM_LIMIT = 56 * 1024 * 1024
F32 = jnp.float32
BF16 = jnp.bfloat16
HI = lax.Precision.HIGHEST


def _cparams(sem):
    return pltpu.CompilerParams(dimension_semantics=sem, vmem_limit_bytes=VMEM_LIMIT)


def _layer_norm(z, g, b):
    mu = jnp.mean(z, axis=-1, keepdims=True)
    zc = z - mu
    var = jnp.mean(zc * zc, axis=-1, keepdims=True)
    return zc * lax.rsqrt(var + LN_EPS) * g + b


def _rms_norm(z, g):
    return z * lax.rsqrt(jnp.mean(z * z, axis=-1, keepdims=True) + RMS_EPS) * g


def _dot(a, b):
    return jnp.dot(a, b, preferred_element_type=F32)


def _dot_nt(a, b):
    return lax.dot_general(a, b, (((1,), (1,)), ((), ())), preferred_element_type=F32)


def _inproj_kernel(x_ref, wa_ref, wqup_ref, wkn_ref, wv_ref, qng_ref, kvng_ref,
                   cd_ref, sda_ref, sdb_ref, cm_ref, sma_ref, smb_ref,
                   qd_ref, kd_ref, vd_ref, qm_ref, km_ref, vm_ref, su_ref):
    xb = x_ref[...].astype(BF16)
    h = _dot(xb, wa_ref[...])
    cd, sda, sdb = cd_ref[...], sda_ref[...], sdb_ref[...]
    cm, sma, smb = cm_ref[...], sma_ref[...], smb_ref[...]

    def rope_d(blk):
        return blk * cd + pltpu.roll(blk, 8, 1) * sda + pltpu.roll(blk, 120, 1) * sdb

    def rope_m(blk):
        return blk * cm + pltpu.roll(blk, 32, 1) * sma + pltpu.roll(blk, 96, 1) * smb

    q_scale = DIFF_HEAD_DIM ** -0.5 * LOG2E
    for c in range(DIFF_QK_W // 128):
        lo, hi = c * 128, (c + 1) * 128
        qd_ref[:, lo:hi] = (rope_d(h[:, A_DQ + lo:A_DQ + hi]) * q_scale).astype(BF16)
        kd_ref[:, lo:hi] = rope_d(h[:, A_DK + lo:A_DK + hi]).astype(BF16)
    vd_ref[...] = h[:, A_DV:A_MQ].astype(BF16)
    su_ref[...] = h[:, A_SU:A_MKR]

    cq = _rms_norm(h[:, A_MQ:A_MKV], qng_ref[...]).astype(BF16)
    qm = _dot(cq, wqup_ref[...]) * ((MLA_NOPE + MLA_ROPE) ** -0.5 * LOG2E)
    ckv = _rms_norm(h[:, A_MKV:A_SU], kvng_ref[...]).astype(BF16)
    kn = _dot(ckv, wkn_ref[...])
    vm_ref[...] = _dot(ckv, wv_ref[...]).astype(BF16)
    krot = rope_m(h[:, A_MKR:A_END]).astype(BF16)
    for hh in range(MLA_HEADS):
        b0 = hh * MLA_QK_PAD
        qm_ref[:, b0:b0 + 128] = qm[:, b0:b0 + 128].astype(BF16)
        qm_ref[:, b0 + 128:b0 + 256] = rope_m(qm[:, b0 + 128:b0 + 256]).astype(BF16)
        km_ref[:, b0:b0 + 128] = kn[:, hh * 128:(hh + 1) * 128].astype(BF16)
        km_ref[:, b0 + 128:b0 + 256] = krot


def _inproj(x, wa, wqup, wkn, wv, qng, kvng, tabs, tm=256):
    s = x.shape[0]
    row = lambda w: pl.BlockSpec((tm, w), lambda i: (i, 0))
    const = lambda a: pl.BlockSpec(a.shape, lambda i: (0,) * a.ndim, pipeline_mode=pl.Buffered(1))
    out_shapes = [jax.ShapeDtypeStruct((s, 1024), BF16), jax.ShapeDtypeStruct((s, 1024), BF16),
                  jax.ShapeDtypeStruct((s, 1024), BF16), jax.ShapeDtypeStruct((s, 2048), BF16),
                  jax.ShapeDtypeStruct((s, 2048), BF16), jax.ShapeDtypeStruct((s, 1024), BF16),
                  jax.ShapeDtypeStruct((s, 1024), F32)]
    return pl.pallas_call(
        _inproj_kernel,
        grid=(s // tm,),
        in_specs=[row(D_MODEL), const(wa), const(wqup), const(wkn), const(wv), const(qng), const(kvng)]
                 + [row(128)] * 6,
        out_specs=[row(1024), row(1024), row(1024), row(2048), row(2048), row(1024), row(1024)],
        out_shape=out_shapes,
        compiler_params=_cparams(("parallel",)),
        name="inproj",
    )(x, wa, wqup, wkn, wv, qng, kvng, *tabs)


def _col_reduce(op, s, ways=8):
    parts = s.reshape(ways, s.shape[0] // ways, s.shape[1])
    return op(op(parts, axis=0), axis=0, keepdims=True)


def _attn_kernel(lam_ref, qt_ref, k_ref, vt_ref, g_ref, o_ref, q2_sc, m_sc, acc_sc, *,
                 tq, tk, cw, diff, post_scale):
    qi = pl.program_id(1)
    if diff:
        qt = qt_ref[...]
        feat = lax.broadcasted_iota(jnp.int32, qt.shape, 0)
        zero = jnp.zeros_like(qt)
        q2_sc[:, :tq] = jnp.where(feat < DIFF_HEAD_DIM, qt, zero)
        q2_sc[:, tq:] = jnp.where(feat >= DIFF_HEAD_DIM, qt, zero)
    else:
        q2_sc[...] = qt_ref[...]
    m_sc[...] = jnp.full(m_sc.shape, NEG_BIG, F32)
    acc_sc[...] = jnp.zeros(acc_sc.shape, F32)
    ratio = tq // tk
    cols = q2_sc.shape[1]

    def step(j, diag):
        off = pl.multiple_of(j * tk, tk)
        kj = k_ref[pl.ds(off, tk), :]
        vj = vt_ref[j]
        q0 = lambda c: (c * cw) % tq
        blocks = [c for c in range(cols // cw) if diag is None or diag * tk < q0(c) + cw]
        nb = len(blocks)
        cs = lambda c: slice(c * cw, (c + 1) * cw)
        qk = lambda c: _dot(kj, q2_sc[:, cs(c)])

        def stats(c, s):
            if diag is not None:
                kpos = lax.broadcasted_iota(jnp.int32, s.shape, 0) + diag * tk
                qpos = lax.broadcasted_iota(jnp.int32, s.shape, 1) + q0(c)
                s = jnp.where(qpos >= kpos, s, NEG_BIG)
            m_old = m_sc[:, cs(c)]
            m_new = jnp.maximum(m_old, _col_reduce(jnp.max, s))
            m_sc[:, cs(c)] = m_new
            return s, m_new, jnp.exp2(m_old - m_new)

        def finish(c, s, m_new, a):
            p = jnp.exp2((s - m_new).astype(BF16))
            acc_sc[:, cs(c)] = a * acc_sc[:, cs(c)] + _dot(vj, p)

        raw = {n: qk(blocks[n]) for n in range(min(2, nb))}
        st = stats(blocks[0], raw.pop(0))
        for n in range(nb):
            if n + 2 < nb:
                raw[n + 2] = qk(blocks[n + 2])
            st_next = stats(blocks[n + 1], raw.pop(n + 1)) if n + 1 < nb else None
            finish(blocks[n], *st)
            st = st_next

    def body(j, carry):
        step(j, None)
        return carry

    lax.fori_loop(0, qi * ratio, body, 0)
    for d in range(ratio):
        step(qi * ratio + d, d)
    ot = acc_sc[:128, :] / acc_sc[128:129, :]
    if diff:
        ot = ot[:, :tq] - lam_ref[0] * ot[:, tq:]
    o = ot.T
    if diff:
        o = _rms_norm(o, g_ref[...]) * post_scale
    o_ref[...] = o.astype(o_ref.dtype)


def _attention(q, k, v, lam, g, *, heads, dqk, diff, post_scale=1.0, tq=512, tk=512, cw=512):
    s = q.shape[0]
    cols = 2 * tq if diff else tq
    nch = s // tk
    qt = q.T
    vt = v.reshape(nch, tk, heads, 128).transpose(2, 0, 3, 1)
    ones = jnp.ones((heads, nch, V_ROWS - 128, tk), BF16)
    vt = jnp.concatenate([vt, ones], axis=2).reshape(heads * nch, V_ROWS, tk)
    kern = functools.partial(_attn_kernel, tq=tq, tk=tk, cw=cw, diff=diff, post_scale=post_scale)
    return pl.pallas_call(
        kern,
        grid=(heads, s // tq),
        in_specs=[pl.BlockSpec(memory_space=pltpu.SMEM),
                  pl.BlockSpec((dqk, tq), lambda h, i: (h, i)),
                  pl.BlockSpec((s, dqk), lambda h, i: (0, h)),
                  pl.BlockSpec((nch, V_ROWS, tk), lambda h, i: (h, 0, 0)),
                  pl.BlockSpec((1, 128), lambda h, i: (0, 0))],
        out_specs=pl.BlockSpec((tq, 128), lambda h, i: (i, h)),
        out_shape=jax.ShapeDtypeStruct((s, heads * 128), BF16),
        scratch_shapes=[pltpu.VMEM((dqk, cols), BF16), pltpu.VMEM((1, cols), F32),
                        pltpu.VMEM((V_ROWS, cols), F32)],
        compiler_params=_cparams(("parallel", "parallel")),
        name="diff_attn" if diff else "mla_attn",
    )(lam, qt, k, vt, g)


def _ssm_kernel(u_ref, wb_ref, ar_ref, ai_ref, wc_ref, d_ref, x0_ref, *refs, tc, emit_y):
    if emit_y:
        y_ref, x_sc, s_sc = refs
    else:
        xe_ref, x_sc, s_sc = refs
    i = pl.program_id(1)
    ns = SSM_BLOCK_STATE

    @pl.when(i == 0)
    def _():
        s_sc[...] = x0_ref[0]

    u = u_ref[...]
    x_sc[...] = _dot(u.astype(BF16), wb_ref[0])
    ar = jnp.broadcast_to(ar_ref[0], (SSM_SEGS, ns))
    ai = jnp.broadcast_to(ai_ref[0], (SSM_SEGS, ns))

    def body(t, carry):
        re, im = carry
        rows = pl.ds(pl.multiple_of(t * SSM_SEGS, SSM_SEGS), SSM_SEGS)
        nre = ar * re - ai * im + x_sc[rows, :ns]
        nim = ar * im + ai * re + x_sc[rows, ns:]
        x_sc[rows, :ns] = nre
        x_sc[rows, ns:] = nim
        return nre, nim

    re, im = lax.fori_loop(0, tc, body, (s_sc[:, :ns], s_sc[:, ns:]), unroll=8)
    s_sc[:, :ns] = re
    s_sc[:, ns:] = im
    if emit_y:
        y = _dot(x_sc[...].astype(BF16), wc_ref[0]) + d_ref[...] * u
        y_ref[...] = jax.nn.gelu(y).astype(y_ref.dtype)
    else:
        @pl.when(i == pl.num_programs(1) - 1)
        def _():
            xe_ref[0] = s_sc[...]


def _ssm_pass(u_perm, p, d, x0, tc, emit_y):
    rows = u_perm.shape[0]
    nchunk = rows // (tc * SSM_SEGS)
    blk3 = lambda a: pl.BlockSpec((1,) + a.shape[1:], lambda j, i: (j, 0, 0))
    if emit_y:
        out_specs = pl.BlockSpec((tc * SSM_SEGS, 128), lambda j, i: (i, j))
        out_shape = jax.ShapeDtypeStruct((rows, SSM_CH), BF16)
    else:
        out_specs = blk3(x0)
        out_shape = jax.ShapeDtypeStruct(x0.shape, F32)
    return pl.pallas_call(
        functools.partial(_ssm_kernel, tc=tc, emit_y=emit_y),
        grid=(SSM_BLOCKS, nchunk),
        in_specs=[pl.BlockSpec((tc * SSM_SEGS, 128), lambda j, i: (i, j)),
                  blk3(p['wb']), blk3(p['ar']), blk3(p['ai']), blk3(p['wc']),
                  pl.BlockSpec((1, 128), lambda j, i: (0, j)), blk3(x0)],
        out_specs=out_specs,
        out_shape=out_shape,
        scratch_shapes=[pltpu.VMEM((tc * SSM_SEGS, 2 * SSM_BLOCK_STATE), F32),
                        pltpu.VMEM((SSM_SEGS, 2 * SSM_BLOCK_STATE), F32)],
        compiler_params=_cparams(("parallel", "arbitrary")),
        name="ssm_scan" if emit_y else "ssm_state",
    )(u_perm, p['wb'], p['ar'], p['ai'], p['wc'], d, x0)


def _ssm_params(lam_re, lam_im, b_re, b_im, c_re, c_im, log_dt, seg_len):
    dt = jnp.exp(log_dt)[:, None]

    def cexp(scale):
        mag = jnp.exp(lam_re * scale)
        return mag * jnp.cos(lam_im * scale), mag * jnp.sin(lam_im * scale)

    ar, ai = cexp(dt)
    sr, si = cexp(dt * seg_len)
    den = lam_re * lam_re + lam_im * lam_im
    fr = ((ar - 1.0) * lam_re + ai * lam_im) / den
    fi = (ai * lam_re - (ar - 1.0) * lam_im) / den
    bbr = fr[..., None] * b_re - fi[..., None] * b_im
    bbi = fr[..., None] * b_im + fi[..., None] * b_re
    eye = jnp.eye(SSM_BLOCKS, dtype=F32)

    def in_mat(w):
        w = w.reshape(SSM_BLOCKS, 8, SSM_STATE, SSM_GROUP)
        return jnp.einsum('jgnp,gh->jgphn', w, eye).reshape(SSM_BLOCKS, 8 * SSM_GROUP, 8 * SSM_STATE)

    def out_mat(w):
        w = w.reshape(SSM_BLOCKS, 8, SSM_GROUP, SSM_STATE)
        return jnp.einsum('jgpn,gh->jgnhp', w, eye).reshape(SSM_BLOCKS, 8 * SSM_STATE, 8 * SSM_GROUP)

    vec = lambda a: a.reshape(SSM_BLOCKS, 1, SSM_BLOCK_STATE)
    return dict(wb=jnp.concatenate([in_mat(bbr), in_mat(bbi)], axis=2).astype(BF16),
                wc=jnp.concatenate([out_mat(c_re), -out_mat(c_im)], axis=1).astype(BF16),
                ar=vec(ar), ai=vec(ai), sr=vec(sr), si=vec(si))


def _ssm(su, p, d, tc):
    s = su.shape[0]
    seg_len = s // SSM_SEGS
    u_perm = su.reshape(SSM_SEGS, seg_len, SSM_CH).transpose(1, 0, 2).reshape(s, SSM_CH)
    ns = SSM_BLOCK_STATE
    zeros = jnp.zeros((SSM_BLOCKS, SSM_SEGS, 2 * ns), F32)
    xe = _ssm_pass(u_perm, p, d, zeros, tc, emit_y=False)
    x0r, x0i = [zeros[:, 0, :ns]], [zeros[:, 0, ns:]]
    for sg in range(SSM_SEGS - 1):
        pr, pi = x0r[-1], x0i[-1]
        x0r.append(p['sr'][:, 0] * pr - p['si'][:, 0] * pi + xe[:, sg, :ns])
        x0i.append(p['sr'][:, 0] * pi + p['si'][:, 0] * pr + xe[:, sg, ns:])
    x0 = jnp.concatenate([jnp.stack(x0r, axis=1), jnp.stack(x0i, axis=1)], axis=2)
    y = _ssm_pass(u_perm, p, d, x0, tc, emit_y=True)
    return y.reshape(seg_len, SSM_SEGS, SSM_CH).transpose(1, 0, 2).reshape(s, SSM_CH)


def _glu_kernel(y_ref, wl_ref, wg_ref, o_ref):
    y = y_ref[...]
    o_ref[...] = (_dot(y, wl_ref[...]) * jax.nn.sigmoid(_dot(y, wg_ref[...]))).astype(o_ref.dtype)


def _glu(y, wl, wg, tm=512):
    s = y.shape[0]
    return pl.pallas_call(
        _glu_kernel,
        grid=(s // tm,),
        in_specs=[pl.BlockSpec((tm, SSM_CH), lambda i: (i, 0)),
                  pl.BlockSpec(wl.shape, lambda i: (0, 0)),
                  pl.BlockSpec(wg.shape, lambda i: (0, 0))],
        out_specs=pl.BlockSpec((tm, SSM_CH), lambda i: (i, 0)),
        out_shape=jax.ShapeDtypeStruct((s, SSM_CH), BF16),
        compiler_params=_cparams(("parallel",)),
        name="ssm_glu",
    )(y, wl, wg)


def _merge_kernel(x_ref, od_ref, om_ref, os_ref, wg_ref, wb_ref, wo_ref, g_ref, b_ref, o_ref, xb_sc):
    j = pl.program_id(1)

    @pl.when(j == 0)
    def _():
        xb_sc[...] = x_ref[...].astype(BF16)
        o_ref[...] = jnp.zeros(o_ref.shape, F32)

    xb = xb_sc[...]
    merged = None
    for n, br_ref in enumerate((od_ref, om_ref, os_ref)):
        term = jax.nn.sigmoid(_dot(xb, wg_ref[n])) * _dot(br_ref[...], wb_ref[n])
        merged = term if merged is None else merged + term
    o_ref[...] += _dot(merged.astype(BF16), wo_ref[...])

    @pl.when(j == pl.num_programs(1) - 1)
    def _():
        o_ref[...] = _layer_norm(ALPHA * x_ref[...] + o_ref[...], g_ref[...], b_ref[...])


def _merge(x, od, om, os_, wg, wb, wo, g, b, tm=512, tn=512):
    s = x.shape[0]
    br = pl.BlockSpec((tm, BRANCH_W), lambda i, j: (i, 0))
    vec = pl.BlockSpec((1, D_MODEL), lambda i, j: (0, 0))
    return pl.pallas_call(
        _merge_kernel,
        grid=(s // tm, D_MODEL // tn),
        in_specs=[pl.BlockSpec((tm, D_MODEL), lambda i, j: (i, 0)), br, br, br,
                  pl.BlockSpec((N_BRANCH, D_MODEL, tn), lambda i, j: (0, 0, j)),
                  pl.BlockSpec((N_BRANCH, BRANCH_W, tn), lambda i, j: (0, 0, j)),
                  pl.BlockSpec((tn, D_MODEL), lambda i, j: (j, 0)), vec, vec],
        out_specs=pl.BlockSpec((tm, D_MODEL), lambda i, j: (i, 0)),
        out_shape=jax.ShapeDtypeStruct((s, D_MODEL), F32),
        scratch_shapes=[pltpu.VMEM((tm, D_MODEL), BF16)],
        compiler_params=_cparams(("parallel", "arbitrary")),
        name="merge_out_ln",
    )(x, od, om, os_, wg, wb, wo, g, b)


def _memkv_kernel(m_ref, w_ref, o_ref):
    o_ref[...] = _dot(m_ref[...].astype(BF16), w_ref[...]).astype(o_ref.dtype)


def _memkv(mem, w):
    return pl.pallas_call(
        _memkv_kernel,
        out_shape=jax.ShapeDtypeStruct((mem.shape[0], w.shape[1]), BF16),
        compiler_params=pltpu.CompilerParams(vmem_limit_bytes=VMEM_LIMIT),
        name="mem_kv",
    )(mem, w)


def _xattn_kernel(x_ref, wq_ref, kv_ref, wo_ref, g_ref, b_ref, o_ref):
    x = x_ref[...]
    q = (_dot(x.astype(BF16), wq_ref[...]) * (X_HEAD_DIM ** -0.5)).astype(BF16)
    outs = []
    for h in range(X_HEADS):
        lo, hi = h * X_HEAD_DIM, (h + 1) * X_HEAD_DIM
        s = _dot_nt(q[:, lo:hi], kv_ref[:, lo:hi])
        p = jnp.exp(s - jnp.max(s, axis=1, keepdims=True))
        o = _dot(p.astype(BF16), kv_ref[:, X_W + lo:X_W + hi])
        outs.append((o / jnp.sum(p, axis=1, keepdims=True)).astype(BF16))
    xa = _dot(jnp.concatenate(outs, axis=1), wo_ref[...])
    o_ref[...] = _layer_norm(ALPHA * x + xa, g_ref[...], b_ref[...])


def _xattn(x, wq, kv, wo, g, b, tm=512):
    s = x.shape[0]
    full = lambda a: pl.BlockSpec(a.shape, lambda i: (0, 0))
    return pl.pallas_call(
        _xattn_kernel,
        grid=(s // tm,),
        in_specs=[pl.BlockSpec((tm, D_MODEL), lambda i: (i, 0)), full(wq), full(kv), full(wo), full(g), full(b)],
        out_specs=pl.BlockSpec((tm, D_MODEL), lambda i: (i, 0)),
        out_shape=jax.ShapeDtypeStruct((s, D_MODEL), F32),
        compiler_params=_cparams(("parallel",)),
        name="xattn_ln",
    )(x, wq, kv, wo, g, b)


def _ffn_kernel(x_ref, wg_ref, wu_ref, wd_ref, g_ref, b_ref, o_ref, xb_sc):
    f = pl.program_id(1)

    @pl.when(f == 0)
    def _():
        xb_sc[...] = x_ref[...].astype(BF16)
        o_ref[...] = jnp.zeros(o_ref.shape, F32)

    xb = xb_sc[...]
    a = jax.nn.silu(_dot(xb, wg_ref[...])) * _dot(xb, wu_ref[...])
    o_ref[...] += _dot(a.astype(BF16), wd_ref[...])

    @pl.when(f == pl.num_programs(1) - 1)
    def _():
        o_ref[...] = _layer_norm(ALPHA * x_ref[...] + o_ref[...], g_ref[...], b_ref[...])


def _ffn(x, wgu, wd, g, b, tm=512, tf=512):
    s = x.shape[0]
    nf = D_FF // tf
    vec = pl.BlockSpec((1, D_MODEL), lambda i, f: (0, 0))
    return pl.pallas_call(
        _ffn_kernel,
        grid=(s // tm, nf),
        in_specs=[pl.BlockSpec((tm, D_MODEL), lambda i, f: (i, 0)),
                  pl.BlockSpec((D_MODEL, tf), lambda i, f: (0, f)),
                  pl.BlockSpec((D_MODEL, tf), lambda i, f: (0, nf + f)),
                  pl.BlockSpec((tf, D_MODEL), lambda i, f: (f, 0)), vec, vec],
        out_specs=pl.BlockSpec((tm, D_MODEL), lambda i, f: (i, 0)),
        out_shape=jax.ShapeDtypeStruct((s, D_MODEL), F32),
        scratch_shapes=[pltpu.VMEM((tm, D_MODEL), BF16)],
        compiler_params=_cparams(("parallel", "arbitrary")),
        name="ffn_ln",
    )(x, wgu, wgu, wd, g, b)


def _router_kernel(x_ref, w_ref, wt_ref, ix_ref):
    logits = jnp.dot(x_ref[...], w_ref[...], precision=HI, preferred_element_type=F32)
    lane = lax.broadcasted_iota(jnp.int32, logits.shape, 1)
    logits = jnp.where(lane < N_EXPERTS, logits, NEG_BIG)
    v1 = jnp.max(logits, axis=1, keepdims=True)
    i1 = jnp.min(jnp.where(logits == v1, lane, 128), axis=1, keepdims=True)
    rest = jnp.where(lane == i1, NEG_BIG, logits)
    v2 = jnp.max(rest, axis=1, keepdims=True)
    i2 = jnp.min(jnp.where(rest == v2, lane, 128), axis=1, keepdims=True)
    e2 = jnp.exp(v2 - v1)
    w1 = 1.0 / (1.0 + e2)
    w2 = e2 / (1.0 + e2)
    wt_ref[...] = jnp.where(lane == 0, w1, jnp.where(lane == 1, w2, 0.0))
    ix_ref[...] = jnp.where(lane == 0, i1, jnp.where(lane == 1, i2, 0))


def _router(x, w_pad, tm=512):
    s = x.shape[0]
    out = pl.BlockSpec((tm, 128), lambda i: (i, 0))
    return pl.pallas_call(
        _router_kernel,
        grid=(s // tm,),
        in_specs=[pl.BlockSpec((tm, D_MODEL), lambda i: (i, 0)), pl.BlockSpec(w_pad.shape, lambda i: (0, 0))],
        out_specs=[out, out],
        out_shape=[jax.ShapeDtypeStruct((s, 128), F32), jax.ShapeDtypeStruct((s, 128), jnp.int32)],
        compiler_params=_cparams(("parallel",)),
        name="router_top2",
    )(x, w_pad)


def _moe_plan(expert_idx, tm):
    n_pairs = expert_idx.shape[0] * 2
    n_tiles = n_pairs // tm + N_EXPERTS
    ef = expert_idx.reshape(-1)
    onehot = (ef[:, None] == jnp.arange(N_EXPERTS)[None, :]).astype(jnp.int32)
    rank = jnp.take_along_axis(jnp.cumsum(onehot, axis=0) - onehot, ef[:, None], axis=1)[:, 0]
    padded = (jnp.sum(onehot, axis=0) + tm - 1) // tm * tm
    ends = jnp.cumsum(padded)
    dest = (ends - padded)[ef] + rank
    src = jnp.zeros((n_tiles * tm,), jnp.int32).at[dest].set(jnp.arange(n_pairs, dtype=jnp.int32) // 2)
    n_valid = ends[-1] // tm
    tile_start = jnp.minimum(jnp.arange(n_tiles), n_valid - 1) * tm
    tile_expert = jnp.sum((tile_start[:, None] >= ends[None, :]).astype(jnp.int32), axis=1)
    return src.reshape(n_tiles, 1, tm), dest, tile_expert.astype(jnp.int32), n_valid.reshape(1).astype(jnp.int32)


def _row_copy(src_hbm, row, dst_vmem, r, sem):
    return pltpu.make_async_copy(src_hbm.at[pl.ds(row, 1), :], dst_vmem.at[pl.ds(r, 1), :], sem)


def _gather_start(src_hbm, row_of, dst_vmem, sem, n):
    def start(r, carry):
        _row_copy(src_hbm, row_of(r), dst_vmem, r, sem).start()
        return carry

    lax.fori_loop(0, n, start, 0, unroll=8)


def _gather_wait(src_hbm, dst_vmem, sem, n):
    def wait(r, carry):
        _row_copy(src_hbm, 0, dst_vmem, r, sem).wait()
        return carry

    lax.fori_loop(0, n, wait, 0, unroll=8)


def _moe_ffn_kernel(te_ref, nv_ref, src_ref, nxt_ref, x_hbm, wg_ref, wu_ref, wd_ref, o_ref,
                    xg_sc, xb_sc, sem, *, tm):
    i, f = pl.program_id(0), pl.program_id(1)
    n_valid = nv_ref[0]
    valid = i < n_valid
    slot = i % 2

    @pl.when(f == 0)
    def _():
        o_ref[...] = jnp.zeros(o_ref.shape, F32)

    @pl.when((i == 0) & (f == 0))
    def _():
        _gather_start(x_hbm, lambda r: src_ref[0, 0, r], xg_sc.at[0], sem.at[0], tm)

    @pl.when(valid & (f == 0))
    def _():
        _gather_wait(x_hbm, xg_sc.at[slot], sem.at[slot], tm)

        @pl.when(i + 1 < n_valid)
        def _():
            _gather_start(x_hbm, lambda r: nxt_ref[0, 0, r], xg_sc.at[1 - slot], sem.at[1 - slot], tm)

        xb_sc[...] = xg_sc[slot].astype(BF16)

    @pl.when(valid)
    def _():
        xb = xb_sc[...]
        a = jax.nn.silu(_dot(xb, wg_ref[0])) * _dot(xb, wu_ref[0])
        o_ref[...] += _dot(a.astype(BF16), wd_ref[0])


def _moe_ffn(x, src, tile_expert, n_valid, wgu, wd, tm, tf=512):
    n_tiles = src.shape[0]
    nf = D_FF // tf
    fcol = lambda i, f, nv: jnp.where(i < nv[0], f, nf - 1)
    grid_spec = pltpu.PrefetchScalarGridSpec(
        num_scalar_prefetch=2,
        grid=(n_tiles, nf),
        in_specs=[pl.BlockSpec((1, 1, tm), lambda i, f, te, nv: (i, 0, 0), memory_space=pltpu.SMEM),
                  pl.BlockSpec((1, 1, tm), lambda i, f, te, nv: (jnp.minimum(i + 1, n_tiles - 1), 0, 0),
                               memory_space=pltpu.SMEM),
                  pl.BlockSpec(memory_space=pl.ANY),
                  pl.BlockSpec((1, D_MODEL, tf), lambda i, f, te, nv: (te[i], 0, fcol(i, f, nv))),
                  pl.BlockSpec((1, D_MODEL, tf), lambda i, f, te, nv: (te[i], 0, nf + fcol(i, f, nv))),
                  pl.BlockSpec((1, tf, D_MODEL), lambda i, f, te, nv: (te[i], fcol(i, f, nv), 0))],
        out_specs=pl.BlockSpec((tm, D_MODEL), lambda i, f, te, nv: (i, 0)),
        scratch_shapes=[pltpu.VMEM((2, tm, D_MODEL), F32), pltpu.VMEM((tm, D_MODEL), BF16),
                        pltpu.SemaphoreType.DMA((2,))])
    return pl.pallas_call(
        functools.partial(_moe_ffn_kernel, tm=tm),
        grid_spec=grid_spec,
        out_shape=jax.ShapeDtypeStruct((n_tiles * tm, D_MODEL), F32),
        compiler_params=_cparams(("arbitrary", "arbitrary")),
        name="moe_ffn",
    )(tile_expert, n_valid, src, src, x, wgu, wgu, wd)


def _moe_combine_kernel(p1_ref, p2_ref, x_ref, wt_ref, y_hbm, g_ref, b_ref, o_ref, y1_sc, y2_sc, sem, *, tm):
    _gather_start(y_hbm, lambda r: p1_ref[0, 0, r], y1_sc, sem.at[0], tm)
    _gather_start(y_hbm, lambda r: p2_ref[0, 0, r], y2_sc, sem.at[1], tm)
    _gather_wait(y_hbm, y1_sc, sem.at[0], tm)
    _gather_wait(y_hbm, y2_sc, sem.at[1], tm)
    wt = wt_ref[...]
    f = wt[:, 0:1] * y1_sc[...] + wt[:, 1:2] * y2_sc[...]
    o_ref[...] = _layer_norm(ALPHA * x_ref[...] + f, g_ref[...], b_ref[...])


def _moe_combine(x, wts, y_sorted, dest, g, b, tm=256):
    s = x.shape[0]
    pos = dest.reshape(s // tm, tm, 2)
    p1, p2 = pos[:, None, :, 0], pos[:, None, :, 1]
    smem = pl.BlockSpec((1, 1, tm), lambda i: (i, 0, 0), memory_space=pltpu.SMEM)
    vec = pl.BlockSpec((1, D_MODEL), lambda i: (0, 0))
    return pl.pallas_call(
        functools.partial(_moe_combine_kernel, tm=tm),
        grid=(s // tm,),
        in_specs=[smem, smem, pl.BlockSpec((tm, D_MODEL), lambda i: (i, 0)),
                  pl.BlockSpec((tm, 128), lambda i: (i, 0)), pl.BlockSpec(memory_space=pl.ANY), vec, vec],
        out_specs=pl.BlockSpec((tm, D_MODEL), lambda i: (i, 0)),
        out_shape=jax.ShapeDtypeStruct((s, D_MODEL), F32),
        scratch_shapes=[pltpu.VMEM((tm, D_MODEL), F32), pltpu.VMEM((tm, D_MODEL), F32),
                        pltpu.SemaphoreType.DMA((2,))],
        compiler_params=_cparams(("arbitrary",)),
        name="moe_combine_ln",
    )(p1, p2, x, wts, y_sorted, g, b)


def _rope_tables(positions):
    pos = positions.astype(F32)[:, None]
    lane = jnp.arange(128) % 64

    def tables(rot):
        half = rot // 2
        inv_freq = jnp.power(jnp.float32(ROPE_THETA), -jnp.arange(0, rot, 2, dtype=F32) / rot)
        ang = pos * inv_freq[None, :]
        cos = jnp.take(jnp.cos(ang), lane % half, axis=1)
        sin = jnp.take(jnp.sin(ang), lane % half, axis=1)
        first, second = (lane < half)[None, :], ((lane >= half) & (lane < rot))[None, :]
        c = jnp.where(first | second, cos, 1.0)
        s_prev = jnp.where(second, sin, 0.0)
        s_next = jnp.where(first, -sin, 0.0)
        return c, s_prev, s_next

    return tables(DIFF_ROT_DIM) + tables(MLA_ROPE)


def _pack_layer(l, w_in, w_q_up, w_kv_up, q_norm_g, kv_norm_g, w_branch, w_out, ssm_glu_w, w_xq, w_xkv, w_xo):
    wi = w_in[l]
    parts = [wi[:, 0:3072], wi[:, 3072:3584], wi[:, 3584:3840], wi[:, 3904:4928], wi[:, 3840:3904],
             jnp.zeros((D_MODEL, 64), F32)]
    wa = jnp.concatenate(parts, axis=1).astype(BF16)
    wqup = w_q_up[l].reshape(MLA_Q_RANK, MLA_HEADS, MLA_NOPE + MLA_ROPE)
    wqup = jnp.pad(wqup, ((0, 0), (0, 0), (0, MLA_QK_PAD - MLA_NOPE - MLA_ROPE)))
    wqup = wqup.reshape(MLA_Q_RANK, MLA_HEADS * MLA_QK_PAD).astype(BF16)
    wkv = w_kv_up[l].reshape(MLA_KV_RANK, MLA_HEADS, MLA_NOPE + MLA_V)
    wkn = wkv[:, :, :MLA_NOPE].reshape(MLA_KV_RANK, MLA_HEADS * MLA_NOPE).astype(BF16)
    wv = wkv[:, :, MLA_NOPE:].reshape(MLA_KV_RANK, MLA_HEADS * MLA_V).astype(BF16)
    wgate = wi[:, 4928:].reshape(D_MODEL, N_BRANCH, D_MODEL).transpose(1, 0, 2).astype(BF16)
    return dict(wa=wa, wqup=wqup, wkn=wkn, wv=wv,
                qng=q_norm_g[l][None, :], kvng=kv_norm_g[l][None, :],
                wgate=wgate, wb=w_branch[l].astype(BF16), wo=w_out[l].astype(BF16),
                wl=ssm_glu_w[l][:, :SSM_CH].astype(BF16), wgl=ssm_glu_w[l][:, SSM_CH:].astype(BF16),
                wxq=w_xq[l].astype(BF16), wxkv=w_xkv[l].astype(BF16), wxo=w_xo[l].astype(BF16))


def _forward(x, mem, positions, w_in, w_q_up, w_kv_up, q_norm_g, kv_norm_g, diff_lambda, diff_subln_g,
             ssm_lam_re, ssm_lam_im, ssm_b_re, ssm_b_im, ssm_c_re, ssm_c_im, ssm_d, ssm_log_dt, ssm_glu_w,
             w_branch, w_out, ln_mix_g, ln_mix_b, w_xq, w_xkv, w_xo, ln_x_g, ln_x_b, ffn_w_gate_up,
             ffn_w_down, moe_router, moe_w_gate_up, moe_w_down, ln_ffn_g, ln_ffn_b, *, ssm_tc, attn_tq):
    s = x.shape[0]
    tabs = _rope_tables(positions)
    row = lambda v: v[None, :]
    ones_g = jnp.ones((1, 128), F32)
    moe_tm = min(512, s // 4)
    for l in range(DEPTH):
        pk = _pack_layer(l, w_in, w_q_up, w_kv_up, q_norm_g, kv_norm_g, w_branch, w_out, ssm_glu_w,
                         w_xq, w_xkv, w_xo)
        qd, kd, vd, qm, km, vm, su = _inproj(x, pk['wa'], pk['wqup'], pk['wkn'], pk['wv'], pk['qng'],
                                             pk['kvng'], tabs)
        lam_init = 0.8 - 0.6 * math.exp(-0.3 * l)
        lf = diff_lambda[l]
        lam = (jnp.exp(jnp.sum(lf[0] * lf[1])) - jnp.exp(jnp.sum(lf[2] * lf[3])) + lam_init).reshape(1)
        o_diff = _attention(qd, kd, vd, lam, row(diff_subln_g[l]), heads=DIFF_HEADS, dqk=128, diff=True,
                            post_scale=1.0 - lam_init, tq=min(2 * attn_tq, s), tk=min(2 * attn_tq, s))
        o_mla = _attention(qm, km, vm, lam, ones_g, heads=MLA_HEADS, dqk=MLA_QK_PAD, diff=False,
                           tq=min(4 * attn_tq, s), tk=min(2 * attn_tq, s))
        sp = _ssm_params(ssm_lam_re[l], ssm_lam_im[l], ssm_b_re[l], ssm_b_im[l], ssm_c_re[l], ssm_c_im[l],
                         ssm_log_dt[l], s // SSM_SEGS)
        y = _ssm(su, sp, row(ssm_d[l]), ssm_tc)
        o_ssm = _glu(y, pk['wl'], pk['wgl'])
        x = _merge(x, o_diff, o_mla, o_ssm, pk['wgate'], pk['wb'], pk['wo'], row(ln_mix_g[l]), row(ln_mix_b[l]))
        kv = _memkv(mem, pk['wxkv'])
        x = _xattn(x, pk['wxq'], kv, pk['wxo'], row(ln_x_g[l]), row(ln_x_b[l]))
        if l % 2 == 0:
            x = _ffn(x, ffn_w_gate_up[l // 2].astype(BF16), ffn_w_down[l // 2].astype(BF16),
                     row(ln_ffn_g[l]), row(ln_ffn_b[l]))
        else:
            wr = jnp.pad(moe_router[l // 2], ((0, 0), (0, 128 - N_EXPERTS)))
            wts, idx = _router(x, wr)
            src, dest, tile_expert, n_valid = _moe_plan(idx[:, :2], moe_tm)
            y_sorted = _moe_ffn(x, src, tile_expert, n_valid, moe_w_gate_up[l // 2].astype(BF16),
                                moe_w_down[l // 2].astype(BF16), moe_tm)
            x = _moe_combine(x, wts, y_sorted, dest, row(ln_ffn_g[l]), row(ln_ffn_b[l]))
    return x


def kernel(x, mem, positions, w_in, w_q_up, w_kv_up, q_norm_g, kv_norm_g, diff_lambda, diff_subln_g, ssm_lam_re, ssm_lam_im, ssm_b_re, ssm_b_im, ssm_c_re, ssm_c_im, ssm_d, ssm_log_dt, ssm_glu_w, w_branch, w_out, ln_mix_g, ln_mix_b, w_xq, w_xkv, w_xo, ln_x_g, ln_x_b, ffn_w_gate_up, ffn_w_down, moe_router, moe_w_gate_up, moe_w_down, ln_ffn_g, ln_ffn_b):
    assert x.shape[0] == 1 and mem.shape[0] == 1
    seq = x.shape[1]
    out = _forward(x[0], mem[0], positions[0], w_in, w_q_up, w_kv_up, q_norm_g, kv_norm_g, diff_lambda,
                   diff_subln_g, ssm_lam_re, ssm_lam_im, ssm_b_re, ssm_b_im, ssm_c_re, ssm_c_im, ssm_d,
                   ssm_log_dt, ssm_glu_w, w_branch, w_out, ln_mix_g, ln_mix_b, w_xq, w_xkv, w_xo, ln_x_g,
                   ln_x_b, ffn_w_gate_up, ffn_w_down, moe_router, moe_w_gate_up, moe_w_down, ln_ffn_g,
                   ln_ffn_b, ssm_tc=min(256, seq // SSM_SEGS), attn_tq=min(512, seq))
    return out[None]
```

```python
import functools
import math

import jax
import jax.numpy as jnp
from jax import lax
from jax.experimental import pallas as pl
from jax.experimental.pallas import tpu as pltpu

D_MODEL = 2048
DEPTH = 2
MEM_LEN = 256
DIFF_HEADS = 8
DIFF_HEAD_DIM = 64
DIFF_V_DIM = 128
DIFF_ROT_DIM = 16
DIFF_QK_W = 1024
DIFF_V_W = 1024
MLA_HEADS = 8
MLA_Q_RANK = 512
MLA_KV_RANK = 256
MLA_NOPE = 128
MLA_ROPE = 64
MLA_V = 128
MLA_QK_PAD = 256
V_ROWS = 144
SSM_CH = 1024
SSM_GROUP = 16
SSM_GROUPS = 64
SSM_STATE = 64
SSM_SEGS = 8
SSM_BLOCKS = 8
SSM_BLOCK_STATE = 512
N_BRANCH = 3
BRANCH_W = 1024
X_HEADS = 4
X_HEAD_DIM = 128
X_W = 512
D_FF = 5632
N_EXPERTS = 8
ROPE_THETA = 500000.0
LN_EPS = 1e-5
RMS_EPS = 1e-6
NEG_BIG = -1e30
ALPHA = (2 * DEPTH) ** 0.25
LOG2E = math.log2(math.e)

A_DQ, A_DK, A_DV, A_MQ, A_MKV, A_SU, A_MKR, A_END = 0, 1024, 2048, 3072, 3584, 3840, 4864, 4992

VMEM_LIMIT = 56 * 1024 * 1024
F32 = jnp.float32
BF16 = jnp.bfloat16
HI = lax.Precision.HIGHEST


def _cparams(sem):
    return pltpu.CompilerParams(dimension_semantics=sem, vmem_limit_bytes=VMEM_LIMIT)


def _layer_norm(z, g, b):
    mu = jnp.mean(z, axis=-1, keepdims=True)
    zc = z - mu
    var = jnp.mean(zc * zc, axis=-1, keepdims=True)
    return zc * lax.rsqrt(var + LN_EPS) * g + b


def _rms_norm(z, g):
    return z * lax.rsqrt(jnp.mean(z * z, axis=-1, keepdims=True) + RMS_EPS) * g


def _dot(a, b):
    return jnp.dot(a, b, preferred_element_type=F32)


def _dot_nt(a, b):
    return lax.dot_general(a, b, (((1,), (1,)), ((), ())), preferred_element_type=F32)


def _inproj_kernel(x_ref, wa_ref, wqup_ref, wkn_ref, wv_ref, qng_ref, kvng_ref,
                   cd_ref, sda_ref, sdb_ref, cm_ref, sma_ref, smb_ref,
                   qd_ref, kd_ref, vd_ref, qm_ref, km_ref, vm_ref, su_ref):
    xb = x_ref[...].astype(BF16)
    h = _dot(xb, wa_ref[...])
    cd, sda, sdb = cd_ref[...], sda_ref[...], sdb_ref[...]
    cm, sma, smb = cm_ref[...], sma_ref[...], smb_ref[...]

    def rope_d(blk):
        return blk * cd + pltpu.roll(blk, 8, 1) * sda + pltpu.roll(blk, 120, 1) * sdb

    def rope_m(blk):
        return blk * cm + pltpu.roll(blk, 32, 1) * sma + pltpu.roll(blk, 96, 1) * smb

    q_scale = DIFF_HEAD_DIM ** -0.5 * LOG2E
    for c in range(DIFF_QK_W // 128):
        lo, hi = c * 128, (c + 1) * 128
        qd_ref[:, lo:hi] = (rope_d(h[:, A_DQ + lo:A_DQ + hi]) * q_scale).astype(BF16)
        kd_ref[:, lo:hi] = rope_d(h[:, A_DK + lo:A_DK + hi]).astype(BF16)
    vd_ref[...] = h[:, A_DV:A_MQ].astype(BF16)
    su_ref[...] = h[:, A_SU:A_MKR]

    cq = _rms_norm(h[:, A_MQ:A_MKV], qng_ref[...]).astype(BF16)
    qm = _dot(cq, wqup_ref[...]) * ((MLA_NOPE + MLA_ROPE) ** -0.5 * LOG2E)
    ckv = _rms_norm(h[:, A_MKV:A_SU], kvng_ref[...]).astype(BF16)
    kn = _dot(ckv, wkn_ref[...])
    vm_ref[...] = _dot(ckv, wv_ref[...]).astype(BF16)
    krot = rope_m(h[:, A_MKR:A_END]).astype(BF16)
    for hh in range(MLA_HEADS):
        b0 = hh * MLA_QK_PAD
        qm_ref[:, b0:b0 + 128] = qm[:, b0:b0 + 128].astype(BF16)
        qm_ref[:, b0 + 128:b0 + 256] = rope_m(qm[:, b0 + 128:b0 + 256]).astype(BF16)
        km_ref[:, b0:b0 + 128] = kn[:, hh * 128:(hh + 1) * 128].astype(BF16)
        km_ref[:, b0 + 128:b0 + 256] = krot


def _inproj(x, wa, wqup, wkn, wv, qng, kvng, tabs, tm=256):
    s = x.shape[0]
    row = lambda w: pl.BlockSpec((tm, w), lambda i: (i, 0))
    const = lambda a: pl.BlockSpec(a.shape, lambda i: (0,) * a.ndim, pipeline_mode=pl.Buffered(1))
    out_shapes = [jax.ShapeDtypeStruct((s, 1024), BF16), jax.ShapeDtypeStruct((s, 1024), BF16),
                  jax.ShapeDtypeStruct((s, 1024), BF16), jax.ShapeDtypeStruct((s, 2048), BF16),
                  jax.ShapeDtypeStruct((s, 2048), BF16), jax.ShapeDtypeStruct((s, 1024), BF16),
                  jax.ShapeDtypeStruct((s, 1024), F32)]
    return pl.pallas_call(
        _inproj_kernel,
        grid=(s // tm,),
        in_specs=[row(D_MODEL), const(wa), const(wqup), const(wkn), const(wv), const(qng), const(kvng)]
                 + [row(128)] * 6,
        out_specs=[row(1024), row(1024), row(1024), row(2048), row(2048), row(1024), row(1024)],
        out_shape=out_shapes,
        compiler_params=_cparams(("parallel",)),
        name="inproj",
    )(x, wa, wqup, wkn, wv, qng, kvng, *tabs)


def _col_reduce(op, s, ways=8):
    parts = s.reshape(ways, s.shape[0] // ways, s.shape[1])
    return op(op(parts, axis=0), axis=0, keepdims=True)


def _attn_kernel(lam_ref, qt_ref, k_ref, vt_ref, g_ref, o_ref, q2_sc, m_sc, acc_sc, s_sc, *,
                 tq, tk, cw, diff, post_scale):
    qi = pl.program_id(1)
    if diff:
        qt = qt_ref[...]
        feat = lax.broadcasted_iota(jnp.int32, qt.shape, 0)
        zero = jnp.zeros_like(qt)
        q2_sc[:, :tq] = jnp.where(feat < DIFF_HEAD_DIM, qt, zero)
        q2_sc[:, tq:] = jnp.where(feat >= DIFF_HEAD_DIM, qt, zero)
    else:
        q2_sc[...] = qt_ref[...]
    m_sc[...] = jnp.full(m_sc.shape, NEG_BIG, F32)
    acc_sc[...] = jnp.zeros(acc_sc.shape, F32)
    ratio = tq // tk
    cols = q2_sc.shape[1]

    def step(j, diag):
        off = pl.multiple_of(j * tk, tk)
        kj = k_ref[pl.ds(off, tk), :]
        vj = vt_ref[j]
        q0 = lambda c: (c * cw) % tq
        blocks = [c for c in range(cols // cw) if diag is None or diag * tk < q0(c) + cw]
        nb = len(blocks)
        cs = lambda c: slice(c * cw, (c + 1) * cw)
        def qk(n):
            c = blocks[n]
            s = _dot(kj, q2_sc[:, cs(c)])
            if diag is not None:
                kpos = lax.broadcasted_iota(jnp.int32, s.shape, 0) + diag * tk
                qpos = lax.broadcasted_iota(jnp.int32, s.shape, 1) + q0(c)
                s = jnp.where(qpos >= kpos, s, NEG_BIG)
            s_sc[n % 3] = s

        def stats(n):
            c = blocks[n]
            m_old = m_sc[:, cs(c)]
            m_new = jnp.maximum(m_old, _col_reduce(jnp.max, s_sc[n % 3]))
            m_sc[:, cs(c)] = m_new
            return m_new, jnp.exp2(m_old - m_new)

        def finish(n, m_new, a):
            c = blocks[n]
            p = jnp.exp2((s_sc[n % 3] - m_new).astype(BF16))
            acc_sc[:, cs(c)] = a * acc_sc[:, cs(c)] + _dot(vj, p)

        for n in range(min(2, nb)):
            qk(n)
        st = stats(0)
        for n in range(nb):
            if n + 2 < nb:
                qk(n + 2)
            st_next = stats(n + 1) if n + 1 < nb else None
            finish(n, *st)
            st = st_next

    def body(j, carry):
        step(j, None)
        return carry

    lax.fori_loop(0, qi * ratio, body, 0)
    for d in range(ratio):
        step(qi * ratio + d, d)
    ot = acc_sc[:128, :] / acc_sc[128:129, :]
    if diff:
        ot = ot[:, :tq] - lam_ref[0] * ot[:, tq:]
    o = ot.T
    if diff:
        o = _rms_norm(o, g_ref[...]) * post_scale
    o_ref[...] = o.astype(o_ref.dtype)


def _attention(q, k, v, lam, g, *, heads, dqk, diff, post_scale=1.0, tq=512, tk=512, cw=512):
    s = q.shape[0]
    cols = 2 * tq if diff else tq
    nch = s // tk
    qt = q.T
    vt = v.reshape(nch, tk, heads, 128).transpose(2, 0, 3, 1)
    ones = jnp.ones((heads, nch, V_ROWS - 128, tk), BF16)
    vt = jnp.concatenate([vt, ones], axis=2).reshape(heads * nch, V_ROWS, tk)
    kern = functools.partial(_attn_kernel, tq=tq, tk=tk, cw=cw, diff=diff, post_scale=post_scale)
    return pl.pallas_call(
        kern,
        grid=(heads, s // tq),
        in_specs=[pl.BlockSpec(memory_space=pltpu.SMEM),
                  pl.BlockSpec((dqk, tq), lambda h, i: (h, i)),
                  pl.BlockSpec((s, dqk), lambda h, i: (0, h)),
                  pl.BlockSpec((nch, V_ROWS, tk), lambda h, i: (h, 0, 0)),
                  pl.BlockSpec((1, 128), lambda h, i: (0, 0))],
        out_specs=pl.BlockSpec((tq, 128), lambda h, i: (i, h)),
        out_shape=jax.ShapeDtypeStruct((s, heads * 128), BF16),
        scratch_shapes=[pltpu.VMEM((dqk, cols), BF16), pltpu.VMEM((1, cols), F32),
                        pltpu.VMEM((V_ROWS, cols), F32), pltpu.VMEM((3, tk, cw), F32)],
        compiler_params=_cparams(("parallel", "parallel")),
        name="diff_attn" if diff else "mla_attn",
    )(lam, qt, k, vt, g)


def _ssm_kernel(u_ref, wb_ref, ar_ref, ai_ref, wc_ref, d_ref, x0_ref, *refs, tc, emit_y):
    if emit_y:
        y_ref, x_sc, s_sc = refs
    else:
        xe_ref, x_sc, s_sc = refs
    i = pl.program_id(1)
    ns = SSM_BLOCK_STATE

    @pl.when(i == 0)
    def _():
        s_sc[...] = x0_ref[0]

    u = u_ref[...]
    x_sc[...] = _dot(u.astype(BF16), wb_ref[0])
    ar = jnp.broadcast_to(ar_ref[0], (SSM_SEGS, ns))
    ai = jnp.broadcast_to(ai_ref[0], (SSM_SEGS, ns))

    def body(t, carry):
        re, im = carry
        rows = pl.ds(pl.multiple_of(t * SSM_SEGS, SSM_SEGS), SSM_SEGS)
        nre = ar * re - ai * im + x_sc[rows, :ns]
        nim = ar * im + ai * re + x_sc[rows, ns:]
        x_sc[rows, :ns] = nre
        x_sc[rows, ns:] = nim
        return nre, nim

    re, im = lax.fori_loop(0, tc, body, (s_sc[:, :ns], s_sc[:, ns:]), unroll=8)
    s_sc[:, :ns] = re
    s_sc[:, ns:] = im
    if emit_y:
        y = _dot(x_sc[...].astype(BF16), wc_ref[0]) + d_ref[...] * u
        y_ref[...] = jax.nn.gelu(y).astype(y_ref.dtype)
    else:
        @pl.when(i == pl.num_programs(1) - 1)
        def _():
            xe_ref[0] = s_sc[...]


def _ssm_pass(u_perm, p, d, x0, tc, emit_y):
    rows = u_perm.shape[0]
    nchunk = rows // (tc * SSM_SEGS)
    blk3 = lambda a: pl.BlockSpec((1,) + a.shape[1:], lambda j, i: (j, 0, 0))
    if emit_y:
        out_specs = pl.BlockSpec((tc * SSM_SEGS, 128), lambda j, i: (i, j))
        out_shape = jax.ShapeDtypeStruct((rows, SSM_CH), BF16)
    else:
        out_specs = blk3(x0)
        out_shape = jax.ShapeDtypeStruct(x0.shape, F32)
    return pl.pallas_call(
        functools.partial(_ssm_kernel, tc=tc, emit_y=emit_y),
        grid=(SSM_BLOCKS, nchunk),
        in_specs=[pl.BlockSpec((tc * SSM_SEGS, 128), lambda j, i: (i, j)),
                  blk3(p['wb']), blk3(p['ar']), blk3(p['ai']), blk3(p['wc']),
                  pl.BlockSpec((1, 128), lambda j, i: (0, j)), blk3(x0)],
        out_specs=out_specs,
        out_shape=out_shape,
        scratch_shapes=[pltpu.VMEM((tc * SSM_SEGS, 2 * SSM_BLOCK_STATE), F32),
                        pltpu.VMEM((SSM_SEGS, 2 * SSM_BLOCK_STATE), F32)],
        compiler_params=_cparams(("parallel", "arbitrary")),
        name="ssm_scan" if emit_y else "ssm_state",
    )(u_perm, p['wb'], p['ar'], p['ai'], p['wc'], d, x0)


def _ssm_params(lam_re, lam_im, b_re, b_im, c_re, c_im, log_dt, seg_len):
    dt = jnp.exp(log_dt)[:, None]

    def cexp(scale):
        mag = jnp.exp(lam_re * scale)
        return mag * jnp.cos(lam_im * scale), mag * jnp.sin(lam_im * scale)

    ar, ai = cexp(dt)
    sr, si = cexp(dt * seg_len)
    den = lam_re * lam_re + lam_im * lam_im
    fr = ((ar - 1.0) * lam_re + ai * lam_im) / den
    fi = (ai * lam_re - (ar - 1.0) * lam_im) / den
    bbr = fr[..., None] * b_re - fi[..., None] * b_im
    bbi = fr[..., None] * b_im + fi[..., None] * b_re
    eye = jnp.eye(SSM_BLOCKS, dtype=F32)

    def in_mat(w):
        w = w.reshape(SSM_BLOCKS, 8, SSM_STATE, SSM_GROUP)
        return jnp.einsum('jgnp,gh->jgphn', w, eye).reshape(SSM_BLOCKS, 8 * SSM_GROUP, 8 * SSM_STATE)

    def out_mat(w):
        w = w.reshape(SSM_BLOCKS, 8, SSM_GROUP, SSM_STATE)
        return jnp.einsum('jgpn,gh->jgnhp', w, eye).reshape(SSM_BLOCKS, 8 * SSM_STATE, 8 * SSM_GROUP)

    vec = lambda a: a.reshape(SSM_BLOCKS, 1, SSM_BLOCK_STATE)
    return dict(wb=jnp.concatenate([in_mat(bbr), in_mat(bbi)], axis=2).astype(BF16),
                wc=jnp.concatenate([out_mat(c_re), -out_mat(c_im)], axis=1).astype(BF16),
                ar=vec(ar), ai=vec(ai), sr=vec(sr), si=vec(si))


def _ssm(su, p, d, tc):
    s = su.shape[0]
    seg_len = s // SSM_SEGS
    u_perm = su.reshape(SSM_SEGS, seg_len, SSM_CH).transpose(1, 0, 2).reshape(s, SSM_CH)
    ns = SSM_BLOCK_STATE
    zeros = jnp.zeros((SSM_BLOCKS, SSM_SEGS, 2 * ns), F32)
    xe = _ssm_pass(u_perm, p, d, zeros, tc, emit_y=False)
    x0r, x0i = [zeros[:, 0, :ns]], [zeros[:, 0, ns:]]
    for sg in range(SSM_SEGS - 1):
        pr, pi = x0r[-1], x0i[-1]
        x0r.append(p['sr'][:, 0] * pr - p['si'][:, 0] * pi + xe[:, sg, :ns])
        x0i.append(p['sr'][:, 0] * pi + p['si'][:, 0] * pr + xe[:, sg, ns:])
    x0 = jnp.concatenate([jnp.stack(x0r, axis=1), jnp.stack(x0i, axis=1)], axis=2)
    y = _ssm_pass(u_perm, p, d, x0, tc, emit_y=True)
    return y.reshape(seg_len, SSM_SEGS, SSM_CH).transpose(1, 0, 2).reshape(s, SSM_CH)


def _glu_kernel(y_ref, wl_ref, wg_ref, o_ref):
    y = y_ref[...]
    o_ref[...] = (_dot(y, wl_ref[...]) * jax.nn.sigmoid(_dot(y, wg_ref[...]))).astype(o_ref.dtype)


def _glu(y, wl, wg, tm=512):
    s = y.shape[0]
    return pl.pallas_call(
        _glu_kernel,
        grid=(s // tm,),
        in_specs=[pl.BlockSpec((tm, SSM_CH), lambda i: (i, 0)),
                  pl.BlockSpec(wl.shape, lambda i: (0, 0)),
                  pl.BlockSpec(wg.shape, lambda i: (0, 0))],
        out_specs=pl.BlockSpec((tm, SSM_CH), lambda i: (i, 0)),
        out_shape=jax.ShapeDtypeStruct((s, SSM_CH), BF16),
        compiler_params=_cparams(("parallel",)),
        name="ssm_glu",
    )(y, wl, wg)


def _merge_kernel(x_ref, od_ref, om_ref, os_ref, wg_ref, wb_ref, wo_ref, g_ref, b_ref, o_ref, xb_sc):
    j = pl.program_id(1)

    @pl.when(j == 0)
    def _():
        xb_sc[...] = x_ref[...].astype(BF16)
        o_ref[...] = jnp.zeros(o_ref.shape, F32)

    xb = xb_sc[...]
    merged = None
    for n, br_ref in enumerate((od_ref, om_ref, os_ref)):
        term = jax.nn.sigmoid(_dot(xb, wg_ref[n])) * _dot(br_ref[...], wb_ref[n])
        merged = term if merged is None else merged + term
    o_ref[...] += _dot(merged.astype(BF16), wo_ref[...])

    @pl.when(j == pl.num_programs(1) - 1)
    def _():
        o_ref[...] = _layer_norm(ALPHA * x_ref[...] + o_ref[...], g_ref[...], b_ref[...])


def _merge(x, od, om, os_, wg, wb, wo, g, b, tm=512, tn=512):
    s = x.shape[0]
    br = pl.BlockSpec((tm, BRANCH_W), lambda i, j: (i, 0))
    vec = pl.BlockSpec((1, D_MODEL), lambda i, j: (0, 0))
    return pl.pallas_call(
        _merge_kernel,
        grid=(s // tm, D_MODEL // tn),
        in_specs=[pl.BlockSpec((tm, D_MODEL), lambda i, j: (i, 0)), br, br, br,
                  pl.BlockSpec((N_BRANCH, D_MODEL, tn), lambda i, j: (0, 0, j)),
                  pl.BlockSpec((N_BRANCH, BRANCH_W, tn), lambda i, j: (0, 0, j)),
                  pl.BlockSpec((tn, D_MODEL), lambda i, j: (j, 0)), vec, vec],
        out_specs=pl.BlockSpec((tm, D_MODEL), lambda i, j: (i, 0)),
        out_shape=jax.ShapeDtypeStruct((s, D_MODEL), F32),
        scratch_shapes=[pltpu.VMEM((tm, D_MODEL), BF16)],
        compiler_params=_cparams(("parallel", "arbitrary")),
        name="merge_out_ln",
    )(x, od, om, os_, wg, wb, wo, g, b)


def _memkv_kernel(m_ref, w_ref, o_ref):
    o_ref[...] = _dot(m_ref[...].astype(BF16), w_ref[...]).astype(o_ref.dtype)


def _memkv(mem, w):
    return pl.pallas_call(
        _memkv_kernel,
        out_shape=jax.ShapeDtypeStruct((mem.shape[0], w.shape[1]), BF16),
        compiler_params=pltpu.CompilerParams(vmem_limit_bytes=VMEM_LIMIT),
        name="mem_kv",
    )(mem, w)


def _xattn_kernel(x_ref, wq_ref, kv_ref, wo_ref, g_ref, b_ref, o_ref):
    x = x_ref[...]
    q = (_dot(x.astype(BF16), wq_ref[...]) * (X_HEAD_DIM ** -0.5)).astype(BF16)
    outs = []
    for h in range(X_HEADS):
        lo, hi = h * X_HEAD_DIM, (h + 1) * X_HEAD_DIM
        s = _dot_nt(q[:, lo:hi], kv_ref[:, lo:hi])
        p = jnp.exp(s - jnp.max(s, axis=1, keepdims=True))
        o = _dot(p.astype(BF16), kv_ref[:, X_W + lo:X_W + hi])
        outs.append((o / jnp.sum(p, axis=1, keepdims=True)).astype(BF16))
    xa = _dot(jnp.concatenate(outs, axis=1), wo_ref[...])
    o_ref[...] = _layer_norm(ALPHA * x + xa, g_ref[...], b_ref[...])


def _xattn(x, wq, kv, wo, g, b, tm=512):
    s = x.shape[0]
    full = lambda a: pl.BlockSpec(a.shape, lambda i: (0, 0))
    return pl.pallas_call(
        _xattn_kernel,
        grid=(s // tm,),
        in_specs=[pl.BlockSpec((tm, D_MODEL), lambda i: (i, 0)), full(wq), full(kv), full(wo), full(g), full(b)],
        out_specs=pl.BlockSpec((tm, D_MODEL), lambda i: (i, 0)),
        out_shape=jax.ShapeDtypeStruct((s, D_MODEL), F32),
        compiler_params=_cparams(("parallel",)),
        name="xattn_ln",
    )(x, wq, kv, wo, g, b)


def _ffn_kernel(x_ref, wg_ref, wu_ref, wd_ref, g_ref, b_ref, o_ref, xb_sc):
    f = pl.program_id(1)

    @pl.when(f == 0)
    def _():
        xb_sc[...] = x_ref[...].astype(BF16)
        o_ref[...] = jnp.zeros(o_ref.shape, F32)

    xb = xb_sc[...]
    a = jax.nn.silu(_dot(xb, wg_ref[...])) * _dot(xb, wu_ref[...])
    o_ref[...] += _dot(a.astype(BF16), wd_ref[...])

    @pl.when(f == pl.num_programs(1) - 1)
    def _():
        o_ref[...] = _layer_norm(ALPHA * x_ref[...] + o_ref[...], g_ref[...], b_ref[...])


def _ffn(x, wgu, wd, g, b, tm=512, tf=512):
    s = x.shape[0]
    nf = D_FF // tf
    vec = pl.BlockSpec((1, D_MODEL), lambda i, f: (0, 0))
    return pl.pallas_call(
        _ffn_kernel,
        grid=(s // tm, nf),
        in_specs=[pl.BlockSpec((tm, D_MODEL), lambda i, f: (i, 0)),
                  pl.BlockSpec((D_MODEL, tf), lambda i, f: (0, f)),
                  pl.BlockSpec((D_MODEL, tf), lambda i, f: (0, nf + f)),
                  pl.BlockSpec((tf, D_MODEL), lambda i, f: (f, 0)), vec, vec],
        out_specs=pl.BlockSpec((tm, D_MODEL), lambda i, f: (i, 0)),
        out_shape=jax.ShapeDtypeStruct((s, D_MODEL), F32),
        scratch_shapes=[pltpu.VMEM((tm, D_MODEL), BF16)],
        compiler_params=_cparams(("parallel", "arbitrary")),
        name="ffn_ln",
    )(x, wgu, wgu, wd, g, b)


def _router_kernel(x_ref, w_ref, wt_ref, ix_ref):
    logits = jnp.dot(x_ref[...], w_ref[...], precision=HI, preferred_element_type=F32)
    lane = lax.broadcasted_iota(jnp.int32, logits.shape, 1)
    logits = jnp.where(lane < N_EXPERTS, logits, NEG_BIG)
    v1 = jnp.max(logits, axis=1, keepdims=True)
    i1 = jnp.min(jnp.where(logits == v1, lane, 128), axis=1, keepdims=True)
    rest = jnp.where(lane == i1, NEG_BIG, logits)
    v2 = jnp.max(rest, axis=1, keepdims=True)
    i2 = jnp.min(jnp.where(rest == v2, lane, 128), axis=1, keepdims=True)
    e2 = jnp.exp(v2 - v1)
    w1 = 1.0 / (1.0 + e2)
    w2 = e2 / (1.0 + e2)
    wt_ref[...] = jnp.where(lane == 0, w1, jnp.where(lane == 1, w2, 0.0))
    ix_ref[...] = jnp.where(lane == 0, i1, jnp.where(lane == 1, i2, 0))


def _router(x, w_pad, tm=512):
    s = x.shape[0]
    out = pl.BlockSpec((tm, 128), lambda i: (i, 0))
    return pl.pallas_call(
        _router_kernel,
        grid=(s // tm,),
        in_specs=[pl.BlockSpec((tm, D_MODEL), lambda i: (i, 0)), pl.BlockSpec(w_pad.shape, lambda i: (0, 0))],
        out_specs=[out, out],
        out_shape=[jax.ShapeDtypeStruct((s, 128), F32), jax.ShapeDtypeStruct((s, 128), jnp.int32)],
        compiler_params=_cparams(("parallel",)),
        name="router_top2",
    )(x, w_pad)


def _moe_plan(expert_idx, tm):
    n_pairs = expert_idx.shape[0] * 2
    n_tiles = n_pairs // tm + N_EXPERTS
    ef = expert_idx.reshape(-1)
    onehot = (ef[:, None] == jnp.arange(N_EXPERTS)[None, :]).astype(jnp.int32)
    rank = jnp.take_along_axis(jnp.cumsum(onehot, axis=0) - onehot, ef[:, None], axis=1)[:, 0]
    padded = (jnp.sum(onehot, axis=0) + tm - 1) // tm * tm
    ends = jnp.cumsum(padded)
    dest = (ends - padded)[ef] + rank
    src = jnp.zeros((n_tiles * tm,), jnp.int32).at[dest].set(jnp.arange(n_pairs, dtype=jnp.int32) // 2)
    n_valid = ends[-1] // tm
    tile_start = jnp.minimum(jnp.arange(n_tiles), n_valid - 1) * tm
    tile_expert = jnp.sum((tile_start[:, None] >= ends[None, :]).astype(jnp.int32), axis=1)
    return src.reshape(n_tiles, 1, tm), dest, tile_expert.astype(jnp.int32), n_valid.reshape(1).astype(jnp.int32)


def _row_copy(src_hbm, row, dst_vmem, r, sem):
    return pltpu.make_async_copy(src_hbm.at[pl.ds(row, 1), :], dst_vmem.at[pl.ds(r, 1), :], sem)


def _gather_start(src_hbm, row_of, dst_vmem, sem, n):
    def start(r, carry):
        _row_copy(src_hbm, row_of(r), dst_vmem, r, sem).start()
        return carry

    lax.fori_loop(0, n, start, 0, unroll=8)


def _gather_wait(src_hbm, dst_vmem, sem, n):
    def wait(r, carry):
        _row_copy(src_hbm, 0, dst_vmem, r, sem).wait()
        return carry

    lax.fori_loop(0, n, wait, 0, unroll=8)


def _moe_ffn_kernel(te_ref, nv_ref, src_ref, nxt_ref, x_hbm, wg_ref, wu_ref, wd_ref, o_ref,
                    xg_sc, xb_sc, sem, *, tm):
    i, f = pl.program_id(0), pl.program_id(1)
    n_valid = nv_ref[0]
    valid = i < n_valid
    slot = i % 2

    @pl.when(f == 0)
    def _():
        o_ref[...] = jnp.zeros(o_ref.shape, F32)

    @pl.when((i == 0) & (f == 0))
    def _():
        _gather_start(x_hbm, lambda r: src_ref[0, 0, r], xg_sc.at[0], sem.at[0], tm)

    @pl.when(valid & (f == 0))
    def _():
        _gather_wait(x_hbm, xg_sc.at[slot], sem.at[slot], tm)

        @pl.when(i + 1 < n_valid)
        def _():
            _gather_start(x_hbm, lambda r: nxt_ref[0, 0, r], xg_sc.at[1 - slot], sem.at[1 - slot], tm)

        xb_sc[...] = xg_sc[slot].astype(BF16)

    @pl.when(valid)
    def _():
        xb = xb_sc[...]
        a = jax.nn.silu(_dot(xb, wg_ref[0])) * _dot(xb, wu_ref[0])
        o_ref[...] += _dot(a.astype(BF16), wd_ref[0])


def _moe_ffn(x, src, tile_expert, n_valid, wgu, wd, tm, tf=512):
    n_tiles = src.shape[0]
    nf = D_FF // tf
    fcol = lambda i, f, nv: jnp.where(i < nv[0], f, nf - 1)
    grid_spec = pltpu.PrefetchScalarGridSpec(
        num_scalar_prefetch=2,
        grid=(n_tiles, nf),
        in_specs=[pl.BlockSpec((1, 1, tm), lambda i, f, te, nv: (i, 0, 0), memory_space=pltpu.SMEM),
                  pl.BlockSpec((1, 1, tm), lambda i, f, te, nv: (jnp.minimum(i + 1, n_tiles - 1), 0, 0),
                               memory_space=pltpu.SMEM),
                  pl.BlockSpec(memory_space=pl.ANY),
                  pl.BlockSpec((1, D_MODEL, tf), lambda i, f, te, nv: (te[i], 0, fcol(i, f, nv))),
                  pl.BlockSpec((1, D_MODEL, tf), lambda i, f, te, nv: (te[i], 0, nf + fcol(i, f, nv))),
                  pl.BlockSpec((1, tf, D_MODEL), lambda i, f, te, nv: (te[i], fcol(i, f, nv), 0))],
        out_specs=pl.BlockSpec((tm, D_MODEL), lambda i, f, te, nv: (i, 0)),
        scratch_shapes=[pltpu.VMEM((2, tm, D_MODEL), F32), pltpu.VMEM((tm, D_MODEL), BF16),
                        pltpu.SemaphoreType.DMA((2,))])
    return pl.pallas_call(
        functools.partial(_moe_ffn_kernel, tm=tm),
        grid_spec=grid_spec,
        out_shape=jax.ShapeDtypeStruct((n_tiles * tm, D_MODEL), F32),
        compiler_params=_cparams(("arbitrary", "arbitrary")),
        name="moe_ffn",
    )(tile_expert, n_valid, src, src, x, wgu, wgu, wd)


def _moe_combine_kernel(p1_ref, p2_ref, x_ref, wt_ref, y_hbm, g_ref, b_ref, o_ref, y1_sc, y2_sc, sem, *, tm):
    _gather_start(y_hbm, lambda r: p1_ref[0, 0, r], y1_sc, sem.at[0], tm)
    _gather_start(y_hbm, lambda r: p2_ref[0, 0, r], y2_sc, sem.at[1], tm)
    _gather_wait(y_hbm, y1_sc, sem.at[0], tm)
    _gather_wait(y_hbm, y2_sc, sem.at[1], tm)
    wt = wt_ref[...]
    f = wt[:, 0:1] * y1_sc[...] + wt[:, 1:2] * y2_sc[...]
    o_ref[...] = _layer_norm(ALPHA * x_ref[...] + f, g_ref[...], b_ref[...])


def _moe_combine(x, wts, y_sorted, dest, g, b, tm=256):
    s = x.shape[0]
    pos = dest.reshape(s // tm, tm, 2)
    p1, p2 = pos[:, None, :, 0], pos[:, None, :, 1]
    smem = pl.BlockSpec((1, 1, tm), lambda i: (i, 0, 0), memory_space=pltpu.SMEM)
    vec = pl.BlockSpec((1, D_MODEL), lambda i: (0, 0))
    return pl.pallas_call(
        functools.partial(_moe_combine_kernel, tm=tm),
        grid=(s // tm,),
        in_specs=[smem, smem, pl.BlockSpec((tm, D_MODEL), lambda i: (i, 0)),
                  pl.BlockSpec((tm, 128), lambda i: (i, 0)), pl.BlockSpec(memory_space=pl.ANY), vec, vec],
        out_specs=pl.BlockSpec((tm, D_MODEL), lambda i: (i, 0)),
        out_shape=jax.ShapeDtypeStruct((s, D_MODEL), F32),
        scratch_shapes=[pltpu.VMEM((tm, D_MODEL), F32), pltpu.VMEM((tm, D_MODEL), F32),
                        pltpu.SemaphoreType.DMA((2,))],
        compiler_params=_cparams(("arbitrary",)),
        name="moe_combine_ln",
    )(p1, p2, x, wts, y_sorted, g, b)


def _rope_tables(positions):
    pos = positions.astype(F32)[:, None]
    lane = jnp.arange(128) % 64

    def tables(rot):
        half = rot // 2
        inv_freq = jnp.power(jnp.float32(ROPE_THETA), -jnp.arange(0, rot, 2, dtype=F32) / rot)
        ang = pos * inv_freq[None, :]
        cos = jnp.take(jnp.cos(ang), lane % half, axis=1)
        sin = jnp.take(jnp.sin(ang), lane % half, axis=1)
        first, second = (lane < half)[None, :], ((lane >= half) & (lane < rot))[None, :]
        c = jnp.where(first | second, cos, 1.0)
        s_prev = jnp.where(second, sin, 0.0)
        s_next = jnp.where(first, -sin, 0.0)
        return c, s_prev, s_next

    return tables(DIFF_ROT_DIM) + tables(MLA_ROPE)


def _pack_layer(l, w_in, w_q_up, w_kv_up, q_norm_g, kv_norm_g, w_branch, w_out, ssm_glu_w, w_xq, w_xkv, w_xo):
    wi = w_in[l]
    parts = [wi[:, 0:3072], wi[:, 3072:3584], wi[:, 3584:3840], wi[:, 3904:4928], wi[:, 3840:3904],
             jnp.zeros((D_MODEL, 64), F32)]
    wa = jnp.concatenate(parts, axis=1).astype(BF16)
    wqup = w_q_up[l].reshape(MLA_Q_RANK, MLA_HEADS, MLA_NOPE + MLA_ROPE)
    wqup = jnp.pad(wqup, ((0, 0), (0, 0), (0, MLA_QK_PAD - MLA_NOPE - MLA_ROPE)))
    wqup = wqup.reshape(MLA_Q_RANK, MLA_HEADS * MLA_QK_PAD).astype(BF16)
    wkv = w_kv_up[l].reshape(MLA_KV_RANK, MLA_HEADS, MLA_NOPE + MLA_V)
    wkn = wkv[:, :, :MLA_NOPE].reshape(MLA_KV_RANK, MLA_HEADS * MLA_NOPE).astype(BF16)
    wv = wkv[:, :, MLA_NOPE:].reshape(MLA_KV_RANK, MLA_HEADS * MLA_V).astype(BF16)
    wgate = wi[:, 4928:].reshape(D_MODEL, N_BRANCH, D_MODEL).transpose(1, 0, 2).astype(BF16)
    return dict(wa=wa, wqup=wqup, wkn=wkn, wv=wv,
                qng=q_norm_g[l][None, :], kvng=kv_norm_g[l][None, :],
                wgate=wgate, wb=w_branch[l].astype(BF16), wo=w_out[l].astype(BF16),
                wl=ssm_glu_w[l][:, :SSM_CH].astype(BF16), wgl=ssm_glu_w[l][:, SSM_CH:].astype(BF16),
                wxq=w_xq[l].astype(BF16), wxkv=w_xkv[l].astype(BF16), wxo=w_xo[l].astype(BF16))


def _forward(x, mem, positions, w_in, w_q_up, w_kv_up, q_norm_g, kv_norm_g, diff_lambda, diff_subln_g,
             ssm_lam_re, ssm_lam_im, ssm_b_re, ssm_b_im, ssm_c_re, ssm_c_im, ssm_d, ssm_log_dt, ssm_glu_w,
             w_branch, w_out, ln_mix_g, ln_mix_b, w_xq, w_xkv, w_xo, ln_x_g, ln_x_b, ffn_w_gate_up,
             ffn_w_down, moe_router, moe_w_gate_up, moe_w_down, ln_ffn_g, ln_ffn_b, *, ssm_tc, attn_tq):
    s = x.shape[0]
    tabs = _rope_tables(positions)
    row = lambda v: v[None, :]
    ones_g = jnp.ones((1, 128), F32)
    moe_tm = min(512, s // 4)
    for l in range(DEPTH):
        pk = _pack_layer(l, w_in, w_q_up, w_kv_up, q_norm_g, kv_norm_g, w_branch, w_out, ssm_glu_w,
                         w_xq, w_xkv, w_xo)
        qd, kd, vd, qm, km, vm, su = _inproj(x, pk['wa'], pk['wqup'], pk['wkn'], pk['wv'], pk['qng'],
                                             pk['kvng'], tabs)
        lam_init = 0.8 - 0.6 * math.exp(-0.3 * l)
        lf = diff_lambda[l]
        lam = (jnp.exp(jnp.sum(lf[0] * lf[1])) - jnp.exp(jnp.sum(lf[2] * lf[3])) + lam_init).reshape(1)
        o_diff = _attention(qd, kd, vd, lam, row(diff_subln_g[l]), heads=DIFF_HEADS, dqk=128, diff=True,
                            post_scale=1.0 - lam_init, tq=min(2 * attn_tq, s), tk=min(2 * attn_tq, s))
        o_mla = _attention(qm, km, vm, lam, ones_g, heads=MLA_HEADS, dqk=MLA_QK_PAD, diff=False,
                           tq=min(4 * attn_tq, s), tk=min(2 * attn_tq, s))
        sp = _ssm_params(ssm_lam_re[l], ssm_lam_im[l], ssm_b_re[l], ssm_b_im[l], ssm_c_re[l], ssm_c_im[l],
                         ssm_log_dt[l], s // SSM_SEGS)
        y = _ssm(su, sp, row(ssm_d[l]), ssm_tc)
        o_ssm = _glu(y, pk['wl'], pk['wgl'])
        x = _merge(x, o_diff, o_mla, o_ssm, pk['wgate'], pk['wb'], pk['wo'], row(ln_mix_g[l]), row(ln_mix_b[l]))
        kv = _memkv(mem, pk['wxkv'])
        x = _xattn(x, pk['wxq'], kv, pk['wxo'], row(ln_x_g[l]), row(ln_x_b[l]))
        if l % 2 == 0:
            x = _ffn(x, ffn_w_gate_up[l // 2].astype(BF16), ffn_w_down[l // 2].astype(BF16),
                     row(ln_ffn_g[l]), row(ln_ffn_b[l]))
        else:
            wr = jnp.pad(moe_router[l // 2], ((0, 0), (0, 128 - N_EXPERTS)))
            wts, idx = _router(x, wr)
            src, dest, tile_expert, n_valid = _moe_plan(idx[:, :2], moe_tm)
            y_sorted = _moe_ffn(x, src, tile_expert, n_valid, moe_w_gate_up[l // 2].astype(BF16),
                                moe_w_down[l // 2].astype(BF16), moe_tm)
            x = _moe_combine(x, wts, y_sorted, dest, row(ln_ffn_g[l]), row(ln_ffn_b[l]))
    return x


def kernel(x, mem, positions, w_in, w_q_up, w_kv_up, q_norm_g, kv_norm_g, diff_lambda, diff_subln_g, ssm_lam_re, ssm_lam_im, ssm_b_re, ssm_b_im, ssm_c_re, ssm_c_im, ssm_d, ssm_log_dt, ssm_glu_w, w_branch, w_out, ln_mix_g, ln_mix_b, w_xq, w_xkv, w_xo, ln_x_g, ln_x_b, ffn_w_gate_up, ffn_w_down, moe_router, moe_w_gate_up, moe_w_down, ln_ffn_g, ln_ffn_b):
    assert x.shape[0] == 1 and mem.shape[0] == 1
    seq = x.shape[1]
    out = _forward(x[0], mem[0], positions[0], w_in, w_q_up, w_kv_up, q_norm_g, kv_norm_g, diff_lambda,
                   diff_subln_g, ssm_lam_re, ssm_lam_im, ssm_b_re, ssm_b_im, ssm_c_re, ssm_c_im, ssm_d,
                   ssm_log_dt, ssm_glu_w, w_branch, w_out, ln_mix_g, ln_mix_b, w_xq, w_xkv, w_xo, ln_x_g,
                   ln_x_b, ffn_w_gate_up, ffn_w_down, moe_router, moe_w_gate_up, moe_w_down, ln_ffn_g,
                   ln_ffn_b, ssm_tc=min(256, seq // SSM_SEGS), attn_tq=min(512, seq))
    return out[None]
```

```python
import functools
import math

import jax
import jax.numpy as jnp
from jax import lax
from jax.experimental import pallas as pl
from jax.experimental.pallas import tpu as pltpu

D_MODEL = 2048
DEPTH = 2
MEM_LEN = 256
DIFF_HEADS = 8
DIFF_HEAD_DIM = 64
DIFF_V_DIM = 128
DIFF_ROT_DIM = 16
DIFF_QK_W = 1024
DIFF_V_W = 1024
MLA_HEADS = 8
MLA_Q_RANK = 512
MLA_KV_RANK = 256
MLA_NOPE = 128
MLA_ROPE = 64
MLA_V = 128
MLA_QK_PAD = 256
V_ROWS = 144
ATTN_SLOTS = 3
SSM_CH = 1024
SSM_GROUP = 16
SSM_GROUPS = 64
SSM_STATE = 64
SSM_SEGS = 8
SSM_BLOCKS = 8
SSM_BLOCK_STATE = 512
N_BRANCH = 3
BRANCH_W = 1024
X_HEADS = 4
X_HEAD_DIM = 128
X_W = 512
D_FF = 5632
N_EXPERTS = 8
ROPE_THETA = 500000.0
LN_EPS = 1e-5
RMS_EPS = 1e-6
NEG_BIG = -1e30
ALPHA = (2 * DEPTH) ** 0.25
LOG2E = math.log2(math.e)

A_DQ, A_DK, A_DV, A_MQ, A_MKV, A_SU, A_MKR, A_END = 0, 1024, 2048, 3072, 3584, 3840, 4864, 4992

VMEM_LIMIT = 56 * 1024 * 1024
F32 = jnp.float32
BF16 = jnp.bfloat16
HI = lax.Precision.HIGHEST


def _cparams(sem):
    return pltpu.CompilerParams(dimension_semantics=sem, vmem_limit_bytes=VMEM_LIMIT)


def _layer_norm(z, g, b):
    mu = jnp.mean(z, axis=-1, keepdims=True)
    zc = z - mu
    var = jnp.mean(zc * zc, axis=-1, keepdims=True)
    return zc * lax.rsqrt(var + LN_EPS) * g + b


def _rms_norm(z, g):
    return z * lax.rsqrt(jnp.mean(z * z, axis=-1, keepdims=True) + RMS_EPS) * g


def _dot(a, b):
    return jnp.dot(a, b, preferred_element_type=F32)


def _dot_nt(a, b):
    return lax.dot_general(a, b, (((1,), (1,)), ((), ())), preferred_element_type=F32)


def _inproj_kernel(x_ref, wa_ref, wqup_ref, wkn_ref, wv_ref, qng_ref, kvng_ref,
                   cd_ref, sda_ref, sdb_ref, cm_ref, sma_ref, smb_ref,
                   qd_ref, kd_ref, vd_ref, qm_ref, km_ref, vm_ref, su_ref):
    xb = x_ref[...].astype(BF16)
    h = _dot(xb, wa_ref[...])
    cd, sda, sdb = cd_ref[...], sda_ref[...], sdb_ref[...]
    cm, sma, smb = cm_ref[...], sma_ref[...], smb_ref[...]

    def rope_d(blk):
        return blk * cd + pltpu.roll(blk, 8, 1) * sda + pltpu.roll(blk, 120, 1) * sdb

    def rope_m(blk):
        return blk * cm + pltpu.roll(blk, 32, 1) * sma + pltpu.roll(blk, 96, 1) * smb

    q_scale = DIFF_HEAD_DIM ** -0.5 * LOG2E
    for c in range(DIFF_QK_W // 128):
        lo, hi = c * 128, (c + 1) * 128
        qd_ref[:, lo:hi] = (rope_d(h[:, A_DQ + lo:A_DQ + hi]) * q_scale).astype(BF16)
        kd_ref[:, lo:hi] = rope_d(h[:, A_DK + lo:A_DK + hi]).astype(BF16)
    vd_ref[...] = h[:, A_DV:A_MQ].astype(BF16)
    su_ref[...] = h[:, A_SU:A_MKR]

    cq = _rms_norm(h[:, A_MQ:A_MKV], qng_ref[...]).astype(BF16)
    qm = _dot(cq, wqup_ref[...]) * ((MLA_NOPE + MLA_ROPE) ** -0.5 * LOG2E)
    ckv = _rms_norm(h[:, A_MKV:A_SU], kvng_ref[...]).astype(BF16)
    kn = _dot(ckv, wkn_ref[...])
    vm_ref[...] = _dot(ckv, wv_ref[...]).astype(BF16)
    krot = rope_m(h[:, A_MKR:A_END]).astype(BF16)
    for hh in range(MLA_HEADS):
        b0 = hh * MLA_QK_PAD
        qm_ref[:, b0:b0 + 128] = qm[:, b0:b0 + 128].astype(BF16)
        qm_ref[:, b0 + 128:b0 + 256] = rope_m(qm[:, b0 + 128:b0 + 256]).astype(BF16)
        km_ref[:, b0:b0 + 128] = kn[:, hh * 128:(hh + 1) * 128].astype(BF16)
        km_ref[:, b0 + 128:b0 + 256] = krot


def _inproj(x, wa, wqup, wkn, wv, qng, kvng, tabs, tm=256):
    s = x.shape[0]
    row = lambda w: pl.BlockSpec((tm, w), lambda i: (i, 0))
    const = lambda a: pl.BlockSpec(a.shape, lambda i: (0,) * a.ndim, pipeline_mode=pl.Buffered(1))
    out_shapes = [jax.ShapeDtypeStruct((s, 1024), BF16), jax.ShapeDtypeStruct((s, 1024), BF16),
                  jax.ShapeDtypeStruct((s, 1024), BF16), jax.ShapeDtypeStruct((s, 2048), BF16),
                  jax.ShapeDtypeStruct((s, 2048), BF16), jax.ShapeDtypeStruct((s, 1024), BF16),
                  jax.ShapeDtypeStruct((s // SSM_SEGS, SSM_SEGS * SSM_CH), F32)]
    seg_tiles = s // SSM_SEGS // tm
    su_spec = pl.BlockSpec((tm, SSM_CH), lambda i: (i % seg_tiles, i // seg_tiles))
    return pl.pallas_call(
        _inproj_kernel,
        grid=(s // tm,),
        in_specs=[row(D_MODEL), const(wa), const(wqup), const(wkn), const(wv), const(qng), const(kvng)]
                 + [row(128)] * 6,
        out_specs=[row(1024), row(1024), row(1024), row(2048), row(2048), row(1024), su_spec],
        out_shape=out_shapes,
        compiler_params=_cparams(("parallel",)),
        name="inproj",
    )(x, wa, wqup, wkn, wv, qng, kvng, *tabs)


def _col_reduce(op, s, ways=8):
    parts = s.reshape(ways, s.shape[0] // ways, s.shape[1])
    return op(op(parts, axis=0), axis=0, keepdims=True)


def _attn_kernel(lam_ref, q_ref, k_ref, vt_ref, g_ref, o_ref, q2_sc, m_sc, acc_sc, s_sc, *,
                 tq, tk, cw, diff, post_scale):
    qi = pl.program_id(1)
    qt = q_ref[...].astype(F32).T.astype(BF16)
    if diff:
        feat = lax.broadcasted_iota(jnp.int32, qt.shape, 0)
        zero = jnp.zeros_like(qt)
        q2_sc[:, :tq] = jnp.where(feat < DIFF_HEAD_DIM, qt, zero)
        q2_sc[:, tq:] = jnp.where(feat >= DIFF_HEAD_DIM, qt, zero)
    else:
        q2_sc[...] = qt
    m_sc[...] = jnp.full(m_sc.shape, NEG_BIG, F32)
    acc_sc[...] = jnp.zeros(acc_sc.shape, F32)
    ratio = tq // tk
    cols = q2_sc.shape[1]

    def step(j, diag):
        off = pl.multiple_of(j * tk, tk)
        kj = k_ref[pl.ds(off, tk), :]
        vj = vt_ref[j]
        q0 = lambda c: (c * cw) % tq
        blocks = [c for c in range(cols // cw) if diag is None or diag * tk < q0(c) + cw]
        nb = len(blocks)
        cs = lambda c: slice(c * cw, (c + 1) * cw)
        slots = s_sc.shape[0]

        def qk(n):
            c = blocks[n]
            s = _dot(kj, q2_sc[:, cs(c)])
            if diag is not None:
                kpos = lax.broadcasted_iota(jnp.int32, s.shape, 0) + diag * tk
                qpos = lax.broadcasted_iota(jnp.int32, s.shape, 1) + q0(c)
                s = jnp.where(qpos >= kpos, s, NEG_BIG)
            s_sc[n % slots] = s
            return _col_reduce(jnp.max, s)

        def stats(n):
            c = blocks[n]
            m_old = m_sc[:, cs(c)]
            m_new = jnp.maximum(m_old, raw.pop(n))
            m_sc[:, cs(c)] = m_new
            return m_new, jnp.exp2(m_old - m_new)

        def finish(n, m_new, a):
            c = blocks[n]
            p = jnp.exp2((s_sc[n % slots] - m_new).astype(BF16))
            acc_sc[:, cs(c)] = a * acc_sc[:, cs(c)] + _dot(vj, p)

        ahead = slots - 1
        raw = {n: qk(n) for n in range(min(ahead, nb))}
        st = stats(0)
        for n in range(nb):
            if n + ahead < nb:
                raw[n + ahead] = qk(n + ahead)
            st_next = stats(n + 1) if n + 1 < nb else None
            finish(n, *st)
            st = st_next

    def body(j, carry):
        step(j, None)
        return carry

    lax.fori_loop(0, qi * ratio, body, 0)
    for d in range(ratio):
        step(qi * ratio + d, d)
    ot = acc_sc[:128, :] / acc_sc[128:129, :]
    if diff:
        ot = ot[:, :tq] - lam_ref[0] * ot[:, tq:]
    o = ot.T
    if diff:
        o = _rms_norm(o, g_ref[...]) * post_scale
    o_ref[...] = o.astype(o_ref.dtype)


def _attention(q, k, v, lam, g, *, heads, dqk, diff, post_scale=1.0, tq=512, tk=512, cw=512):
    s = q.shape[0]
    cols = 2 * tq if diff else tq
    nch = s // tk
    vt =v.reshape(nch, tk, heads, 128).transpose(2, 0, 3, 1)
    ones = jnp.ones((heads, nch, V_ROWS - 128, tk), BF16)
    vt = jnp.concatenate([vt, ones], axis=2).reshape(heads * nch, V_ROWS, tk)
    kern = functools.partial(_attn_kernel, tq=tq, tk=tk, cw=cw, diff=diff, post_scale=post_scale)
    return pl.pallas_call(
        kern,
        grid=(heads, s // tq),
        in_specs=[pl.BlockSpec(memory_space=pltpu.SMEM),
                  pl.BlockSpec((tq, dqk), lambda h, i: (i, h)),
                  pl.BlockSpec((s, dqk), lambda h, i: (0, h)),
                  pl.BlockSpec((nch, V_ROWS, tk), lambda h, i: (h, 0, 0)),
                  pl.BlockSpec((1, 128), lambda h, i: (0, 0))],
        out_specs=pl.BlockSpec((tq, 128), lambda h, i: (i, h)),
        out_shape=jax.ShapeDtypeStruct((s, heads * 128), BF16),
        scratch_shapes=[pltpu.VMEM((dqk, cols), BF16), pltpu.VMEM((1, cols), F32),
                        pltpu.VMEM((V_ROWS, cols), F32), pltpu.VMEM((ATTN_SLOTS, tk, cw), F32)],
        compiler_params=_cparams(("parallel", "parallel")),
        name="diff_attn" if diff else "mla_attn",
    )(lam, q, k, vt, g)


def _ssm_kernel(u_ref, wb_ref, ar_ref, ai_ref, wc_ref, d_ref, x0_ref, *refs, tc, emit_y):
    if emit_y:
        y_ref, x_sc, s_sc = refs
    else:
        xe_ref, x_sc, s_sc = refs
    i = pl.program_id(1)
    ns = SSM_BLOCK_STATE

    @pl.when(i == 0)
    def _():
        s_sc[...] = x0_ref[0]

    u = u_ref[...]
    x_sc[...] = _dot(u.astype(BF16), wb_ref[0])
    ar = jnp.broadcast_to(ar_ref[0], (SSM_SEGS, ns))
    ai = jnp.broadcast_to(ai_ref[0], (SSM_SEGS, ns))

    def body(t, carry):
        re, im = carry
        rows = pl.ds(pl.multiple_of(t * SSM_SEGS, SSM_SEGS), SSM_SEGS)
        nre = ar * re - ai * im + x_sc[rows, :ns]
        nim = ar * im + ai * re + x_sc[rows, ns:]
        x_sc[rows, :ns] = nre
        x_sc[rows, ns:] = nim
        return nre, nim

    re, im = lax.fori_loop(0, tc, body, (s_sc[:, :ns], s_sc[:, ns:]), unroll=8)
    s_sc[:, :ns] = re
    s_sc[:, ns:] = im
    if emit_y:
        y = _dot(x_sc[...].astype(BF16), wc_ref[0]) + d_ref[...] * u
        y_ref[...] = jax.nn.gelu(y).astype(y_ref.dtype)
    else:
        @pl.when(i == pl.num_programs(1) - 1)
        def _():
            xe_ref[0] = s_sc[...]


def _ssm_pass(u_perm, p, d, x0, tc, emit_y):
    rows = u_perm.shape[0]
    nchunk = rows // (tc * SSM_SEGS)
    blk3 = lambda a: pl.BlockSpec((1,) + a.shape[1:], lambda j, i: (j, 0, 0))
    if emit_y:
        out_specs = pl.BlockSpec((tc * SSM_SEGS, 128), lambda j, i: (i, j))
        out_shape = jax.ShapeDtypeStruct((rows, SSM_CH), BF16)
    else:
        out_specs = blk3(x0)
        out_shape = jax.ShapeDtypeStruct(x0.shape, F32)
    return pl.pallas_call(
        functools.partial(_ssm_kernel, tc=tc, emit_y=emit_y),
        grid=(SSM_BLOCKS, nchunk),
        in_specs=[pl.BlockSpec((tc * SSM_SEGS, 128), lambda j, i: (i, j)),
                  blk3(p['wb']), blk3(p['ar']), blk3(p['ai']), blk3(p['wc']),
                  pl.BlockSpec((1, 128), lambda j, i: (0, j)), blk3(x0)],
        out_specs=out_specs,
        out_shape=out_shape,
        scratch_shapes=[pltpu.VMEM((tc * SSM_SEGS, 2 * SSM_BLOCK_STATE), F32),
                        pltpu.VMEM((SSM_SEGS, 2 * SSM_BLOCK_STATE), F32)],
        compiler_params=_cparams(("parallel", "arbitrary")),
        name="ssm_scan" if emit_y else "ssm_state",
    )(u_perm, p['wb'], p['ar'], p['ai'], p['wc'], d, x0)


def _ssm_params(lam_re, lam_im, b_re, b_im, c_re, c_im, log_dt, seg_len):
    dt = jnp.exp(log_dt)[:, None]

    def cexp(scale):
        mag = jnp.exp(lam_re * scale)
        return mag * jnp.cos(lam_im * scale), mag * jnp.sin(lam_im * scale)

    ar, ai = cexp(dt)
    sr, si = cexp(dt * seg_len)
    den = lam_re * lam_re + lam_im * lam_im
    fr = ((ar - 1.0) * lam_re + ai * lam_im) / den
    fi = (ai * lam_re - (ar - 1.0) * lam_im) / den
    bbr = fr[..., None] * b_re - fi[..., None] * b_im
    bbi = fr[..., None] * b_im + fi[..., None] * b_re
    eye = jnp.eye(SSM_BLOCKS, dtype=F32)

    def in_mat(w):
        w = w.reshape(SSM_BLOCKS, 8, SSM_STATE, SSM_GROUP)
        return jnp.einsum('jgnp,gh->jgphn', w, eye).reshape(SSM_BLOCKS, 8 * SSM_GROUP, 8 * SSM_STATE)

    def out_mat(w):
        w = w.reshape(SSM_BLOCKS, 8, SSM_GROUP, SSM_STATE)
        return jnp.einsum('jgpn,gh->jgnhp', w, eye).reshape(SSM_BLOCKS, 8 * SSM_STATE, 8 * SSM_GROUP)

    vec = lambda a: a.reshape(SSM_BLOCKS, 1, SSM_BLOCK_STATE)
    return dict(wb=jnp.concatenate([in_mat(bbr), in_mat(bbi)], axis=2).astype(BF16),
                wc=jnp.concatenate([out_mat(c_re), -out_mat(c_im)], axis=1).astype(BF16),
                ar=vec(ar), ai=vec(ai), sr=vec(sr), si=vec(si))


def _ssm(su_seg, p, d, tc):
    seg_len = su_seg.shape[0]
    s = seg_len * SSM_SEGS
    u_perm = su_seg.reshape(s, SSM_CH)
    ns = SSM_BLOCK_STATE
    zeros = jnp.zeros((SSM_BLOCKS, SSM_SEGS, 2 * ns), F32)
    xe = _ssm_pass(u_perm, p, d, zeros, tc, emit_y=False)
    x0r, x0i = [zeros[:, 0, :ns]], [zeros[:, 0, ns:]]
    for sg in range(SSM_SEGS - 1):
        pr, pi = x0r[-1], x0i[-1]
        x0r.append(p['sr'][:, 0] * pr - p['si'][:, 0] * pi + xe[:, sg, :ns])
        x0i.append(p['sr'][:, 0] * pi + p['si'][:, 0] * pr + xe[:, sg, ns:])
    x0 = jnp.concatenate([jnp.stack(x0r, axis=1), jnp.stack(x0i, axis=1)], axis=2)
    y = _ssm_pass(u_perm, p, d, x0, tc, emit_y=True)
    return y.reshape(seg_len, SSM_SEGS, SSM_CH).transpose(1, 0, 2).reshape(s, SSM_CH)


def _glu_kernel(y_ref, wl_ref, wg_ref, o_ref):
    y = y_ref[...]
    o_ref[...] = (_dot(y, wl_ref[...]) * jax.nn.sigmoid(_dot(y, wg_ref[...]))).astype(o_ref.dtype)


def _glu(y, wl, wg, tm=512):
    s = y.shape[0]
    return pl.pallas_call(
        _glu_kernel,
        grid=(s // tm,),
        in_specs=[pl.BlockSpec((tm, SSM_CH), lambda i: (i, 0)),
                  pl.BlockSpec(wl.shape, lambda i: (0, 0)),
                  pl.BlockSpec(wg.shape, lambda i: (0, 0))],
        out_specs=pl.BlockSpec((tm, SSM_CH), lambda i: (i, 0)),
        out_shape=jax.ShapeDtypeStruct((s, SSM_CH), BF16),
        compiler_params=_cparams(("parallel",)),
        name="ssm_glu",
    )(y, wl, wg)


def _merge_kernel(x_ref, od_ref, om_ref, os_ref, wg_ref, wb_ref, wo_ref, g_ref, b_ref, o_ref, xb_sc):
    j = pl.program_id(1)

    @pl.when(j == 0)
    def _():
        xb_sc[...] = x_ref[...].astype(BF16)
        o_ref[...] = jnp.zeros(o_ref.shape, F32)

    xb = xb_sc[...]
    merged = None
    for n, br_ref in enumerate((od_ref, om_ref, os_ref)):
        term = jax.nn.sigmoid(_dot(xb, wg_ref[n])) * _dot(br_ref[...], wb_ref[n])
        merged = term if merged is None else merged + term
    o_ref[...] += _dot(merged.astype(BF16), wo_ref[...])

    @pl.when(j == pl.num_programs(1) - 1)
    def _():
        o_ref[...] = _layer_norm(ALPHA * x_ref[...] + o_ref[...], g_ref[...], b_ref[...])


def _merge(x, od, om, os_, wg, wb, wo, g, b, tm=512, tn=512):
    s = x.shape[0]
    br = pl.BlockSpec((tm, BRANCH_W), lambda i, j: (i, 0))
    vec = pl.BlockSpec((1, D_MODEL), lambda i, j: (0, 0))
    return pl.pallas_call(
        _merge_kernel,
        grid=(s // tm, D_MODEL // tn),
        in_specs=[pl.BlockSpec((tm, D_MODEL), lambda i, j: (i, 0)), br, br, br,
                  pl.BlockSpec((N_BRANCH, D_MODEL, tn), lambda i, j: (0, 0, j)),
                  pl.BlockSpec((N_BRANCH, BRANCH_W, tn), lambda i, j: (0, 0, j)),
                  pl.BlockSpec((tn, D_MODEL), lambda i, j: (j, 0)), vec, vec],
        out_specs=pl.BlockSpec((tm, D_MODEL), lambda i, j: (i, 0)),
        out_shape=jax.ShapeDtypeStruct((s, D_MODEL), F32),
        scratch_shapes=[pltpu.VMEM((tm, D_MODEL), BF16)],
        compiler_params=_cparams(("parallel", "arbitrary")),
        name="merge_out_ln",
    )(x, od, om, os_, wg, wb, wo, g, b)


def _memkv_kernel(m_ref, w_ref, o_ref):
    o_ref[...] = _dot(m_ref[...].astype(BF16), w_ref[...]).astype(o_ref.dtype)


def _memkv(mem, w):
    return pl.pallas_call(
        _memkv_kernel,
        out_shape=jax.ShapeDtypeStruct((mem.shape[0], w.shape[1]), BF16),
        compiler_params=pltpu.CompilerParams(vmem_limit_bytes=VMEM_LIMIT),
        name="mem_kv",
    )(mem, w)


def _xattn_kernel(x_ref, wq_ref, kv_ref, wo_ref, g_ref, b_ref, o_ref):
    x = x_ref[...]
    q = (_dot(x.astype(BF16), wq_ref[...]) * (X_HEAD_DIM ** -0.5)).astype(BF16)
    outs = []
    for h in range(X_HEADS):
        lo, hi = h * X_HEAD_DIM, (h + 1) * X_HEAD_DIM
        s = _dot_nt(q[:, lo:hi], kv_ref[:, lo:hi])
        p = jnp.exp(s - jnp.max(s, axis=1, keepdims=True))
        o = _dot(p.astype(BF16), kv_ref[:, X_W + lo:X_W + hi])
        outs.append((o / jnp.sum(p, axis=1, keepdims=True)).astype(BF16))
    xa = _dot(jnp.concatenate(outs, axis=1), wo_ref[...])
    o_ref[...] = _layer_norm(ALPHA * x + xa, g_ref[...], b_ref[...])


def _xattn(x, wq, kv, wo, g, b, tm=512):
    s = x.shape[0]
    full = lambda a: pl.BlockSpec(a.shape, lambda i: (0, 0))
    return pl.pallas_call(
        _xattn_kernel,
        grid=(s // tm,),
        in_specs=[pl.BlockSpec((tm, D_MODEL), lambda i: (i, 0)), full(wq), full(kv), full(wo), full(g), full(b)],
        out_specs=pl.BlockSpec((tm, D_MODEL), lambda i: (i, 0)),
        out_shape=jax.ShapeDtypeStruct((s, D_MODEL), F32),
        compiler_params=_cparams(("parallel",)),
        name="xattn_ln",
    )(x, wq, kv, wo, g, b)


def _ffn_kernel(x_ref, wg_ref, wu_ref, wd_ref, g_ref, b_ref, o_ref, xb_sc):
    f = pl.program_id(1)

    @pl.when(f == 0)
    def _():
        xb_sc[...] = x_ref[...].astype(BF16)
        o_ref[...] = jnp.zeros(o_ref.shape, F32)

    xb = xb_sc[...]
    a = jax.nn.silu(_dot(xb, wg_ref[...])) * _dot(xb, wu_ref[...])
    o_ref[...] += _dot(a.astype(BF16), wd_ref[...])

    @pl.when(f == pl.num_programs(1) - 1)
    def _():
        o_ref[...] = _layer_norm(ALPHA * x_ref[...] + o_ref[...], g_ref[...], b_ref[...])


def _ffn(x, wgu, wd, g, b, tm=512, tf=512):
    s = x.shape[0]
    nf = D_FF // tf
    vec = pl.BlockSpec((1, D_MODEL), lambda i, f: (0, 0))
    return pl.pallas_call(
        _ffn_kernel,
        grid=(s // tm, nf),
        in_specs=[pl.BlockSpec((tm, D_MODEL), lambda i, f: (i, 0)),
                  pl.BlockSpec((D_MODEL, tf), lambda i, f: (0, f)),
                  pl.BlockSpec((D_MODEL, tf), lambda i, f: (0, nf + f)),
                  pl.BlockSpec((tf, D_MODEL), lambda i, f: (f, 0)), vec, vec],
        out_specs=pl.BlockSpec((tm, D_MODEL), lambda i, f: (i, 0)),
        out_shape=jax.ShapeDtypeStruct((s, D_MODEL), F32),
        scratch_shapes=[pltpu.VMEM((tm, D_MODEL), BF16)],
        compiler_params=_cparams(("parallel", "arbitrary")),
        name="ffn_ln",
    )(x, wgu, wgu, wd, g, b)


def _router_kernel(x_ref, w_ref, wt_ref, ix_ref):
    logits = jnp.dot(x_ref[...], w_ref[...], precision=HI, preferred_element_type=F32)
    lane = lax.broadcasted_iota(jnp.int32, logits.shape, 1)
    logits = jnp.where(lane < N_EXPERTS, logits, NEG_BIG)
    v1 = jnp.max(logits, axis=1, keepdims=True)
    i1 = jnp.min(jnp.where(logits == v1, lane, 128), axis=1, keepdims=True)
    rest = jnp.where(lane == i1, NEG_BIG, logits)
    v2 = jnp.max(rest, axis=1, keepdims=True)
    i2 = jnp.min(jnp.where(rest == v2, lane, 128), axis=1, keepdims=True)
    e2 = jnp.exp(v2 - v1)
    w1 = 1.0 / (1.0 + e2)
    w2 = e2 / (1.0 + e2)
    wt_ref[...] = jnp.where(lane == 0, w1, jnp.where(lane == 1, w2, 0.0))
    ix_ref[...] = jnp.where(lane == 0, i1, jnp.where(lane == 1, i2, 0))


def _router(x, w_pad, tm=512):
    s = x.shape[0]
    out = pl.BlockSpec((tm, 128), lambda i: (i, 0))
    return pl.pallas_call(
        _router_kernel,
        grid=(s // tm,),
        in_specs=[pl.BlockSpec((tm, D_MODEL), lambda i: (i, 0)), pl.BlockSpec(w_pad.shape, lambda i: (0, 0))],
        out_specs=[out, out],
        out_shape=[jax.ShapeDtypeStruct((s, 128), F32), jax.ShapeDtypeStruct((s, 128), jnp.int32)],
        compiler_params=_cparams(("parallel",)),
        name="router_top2",
    )(x, w_pad)


def _moe_plan(expert_idx, tm):
    n_pairs = expert_idx.shape[0] * 2
    n_tiles = n_pairs // tm + N_EXPERTS
    ef = expert_idx.reshape(-1)
    onehot = (ef[:, None] == jnp.arange(N_EXPERTS)[None, :]).astype(jnp.int32)
    rank = jnp.take_along_axis(jnp.cumsum(onehot, axis=0) - onehot, ef[:, None], axis=1)[:, 0]
    padded = (jnp.sum(onehot, axis=0) + tm - 1) // tm * tm
    ends = jnp.cumsum(padded)
    dest = (ends - padded)[ef] + rank
    src = jnp.zeros((n_tiles * tm,), jnp.int32).at[dest].set(jnp.arange(n_pairs, dtype=jnp.int32) // 2)
    n_valid = ends[-1] // tm
    tile_start = jnp.minimum(jnp.arange(n_tiles), n_valid - 1) * tm
    tile_expert = jnp.sum((tile_start[:, None] >= ends[None, :]).astype(jnp.int32), axis=1)
    return src.reshape(n_tiles, 1, tm), dest, tile_expert.astype(jnp.int32), n_valid.reshape(1).astype(jnp.int32)


def _row_copy(src_hbm, row, dst_vmem, r, sem):
    return pltpu.make_async_copy(src_hbm.at[pl.ds(row, 1), :], dst_vmem.at[pl.ds(r, 1), :], sem)


def _gather_start(src_hbm, row_of, dst_vmem, sem, n):
    def start(r, carry):
        _row_copy(src_hbm, row_of(r), dst_vmem, r, sem).start()
        return carry

    lax.fori_loop(0, n, start, 0, unroll=8)


def _gather_wait(src_hbm, dst_vmem, sem, n):
    def wait(r, carry):
        _row_copy(src_hbm, 0, dst_vmem, r, sem).wait()
        return carry

    lax.fori_loop(0, n, wait, 0, unroll=8)


def _moe_ffn_kernel(te_ref, nv_ref, src_ref, nxt_ref, x_hbm, wg_ref, wu_ref, wd_ref, o_ref,
                    xg_sc, xb_sc, sem, *, tm):
    i, f = pl.program_id(0), pl.program_id(1)
    n_valid = nv_ref[0]
    valid = i < n_valid
    slot = i % 2

    @pl.when(f == 0)
    def _():
        o_ref[...] = jnp.zeros(o_ref.shape, F32)

    @pl.when((i == 0) & (f == 0))
    def _():
        _gather_start(x_hbm, lambda r: src_ref[0, 0, r], xg_sc.at[0], sem.at[0], tm)

    @pl.when(valid & (f == 0))
    def _():
        _gather_wait(x_hbm, xg_sc.at[slot], sem.at[slot], tm)

        @pl.when(i + 1 < n_valid)
        def _():
            _gather_start(x_hbm, lambda r: nxt_ref[0, 0, r], xg_sc.at[1 - slot], sem.at[1 - slot], tm)

        xb_sc[...] = xg_sc[slot].astype(BF16)

    @pl.when(valid)
    def _():
        xb = xb_sc[...]
        a = jax.nn.silu(_dot(xb, wg_ref[0])) * _dot(xb, wu_ref[0])
        o_ref[...] += _dot(a.astype(BF16), wd_ref[0])


def _moe_ffn(x, src, tile_expert, n_valid, wgu, wd, tm, tf=512):
    n_tiles = src.shape[0]
    nf = D_FF // tf
    fcol = lambda i, f, nv: jnp.where(i < nv[0], f, nf - 1)
    grid_spec = pltpu.PrefetchScalarGridSpec(
        num_scalar_prefetch=2,
        grid=(n_tiles, nf),
        in_specs=[pl.BlockSpec((1, 1, tm), lambda i, f, te, nv: (i, 0, 0), memory_space=pltpu.SMEM),
                  pl.BlockSpec((1, 1, tm), lambda i, f, te, nv: (jnp.minimum(i + 1, n_tiles - 1), 0, 0),
                               memory_space=pltpu.SMEM),
                  pl.BlockSpec(memory_space=pl.ANY),
                  pl.BlockSpec((1, D_MODEL, tf), lambda i, f, te, nv: (te[i], 0, fcol(i, f, nv))),
                  pl.BlockSpec((1, D_MODEL, tf), lambda i, f, te, nv: (te[i], 0, nf + fcol(i, f, nv))),
                  pl.BlockSpec((1, tf, D_MODEL), lambda i, f, te, nv: (te[i], fcol(i, f, nv), 0))],
        out_specs=pl.BlockSpec((tm, D_MODEL), lambda i, f, te, nv: (i, 0)),
        scratch_shapes=[pltpu.VMEM((2, tm, D_MODEL), F32), pltpu.VMEM((tm, D_MODEL), BF16),
                        pltpu.SemaphoreType.DMA((2,))])
    return pl.pallas_call(
        functools.partial(_moe_ffn_kernel, tm=tm),
        grid_spec=grid_spec,
        out_shape=jax.ShapeDtypeStruct((n_tiles * tm, D_MODEL), F32),
        compiler_params=_cparams(("arbitrary", "arbitrary")),
        name="moe_ffn",
    )(tile_expert, n_valid, src, src, x, wgu, wgu, wd)


def _moe_combine_kernel(p1_ref, p2_ref, x_ref, wt_ref, y_hbm, g_ref, b_ref, o_ref, y1_sc, y2_sc, sem, *, tm):
    _gather_start(y_hbm, lambda r: p1_ref[0, 0, r], y1_sc, sem.at[0], tm)
    _gather_start(y_hbm, lambda r: p2_ref[0, 0, r], y2_sc, sem.at[1], tm)
    _gather_wait(y_hbm, y1_sc, sem.at[0], tm)
    _gather_wait(y_hbm, y2_sc, sem.at[1], tm)
    wt = wt_ref[...]
    f = wt[:, 0:1] * y1_sc[...] + wt[:, 1:2] * y2_sc[...]
    o_ref[...] = _layer_norm(ALPHA * x_ref[...] + f, g_ref[...], b_ref[...])


def _moe_combine(x, wts, y_sorted, dest, g, b, tm=256):
    s = x.shape[0]
    pos = dest.reshape(s // tm, tm, 2)
    p1, p2 = pos[:, None, :, 0], pos[:, None, :, 1]
    smem = pl.BlockSpec((1, 1, tm), lambda i: (i, 0, 0), memory_space=pltpu.SMEM)
    vec = pl.BlockSpec((1, D_MODEL), lambda i: (0, 0))
    return pl.pallas_call(
        functools.partial(_moe_combine_kernel, tm=tm),
        grid=(s // tm,),
        in_specs=[smem, smem, pl.BlockSpec((tm, D_MODEL), lambda i: (i, 0)),
                  pl.BlockSpec((tm, 128), lambda i: (i, 0)), pl.BlockSpec(memory_space=pl.ANY), vec, vec],
        out_specs=pl.BlockSpec((tm, D_MODEL), lambda i: (i, 0)),
        out_shape=jax.ShapeDtypeStruct((s, D_MODEL), F32),
        scratch_shapes=[pltpu.VMEM((tm, D_MODEL), F32), pltpu.VMEM((tm, D_MODEL), F32),
                        pltpu.SemaphoreType.DMA((2,))],
        compiler_params=_cparams(("arbitrary",)),
        name="moe_combine_ln",
    )(p1, p2, x, wts, y_sorted, g, b)


def _rope_tables(positions):
    pos = positions.astype(F32)[:, None]
    lane = jnp.arange(128) % 64

    def tables(rot):
        half = rot // 2
        inv_freq = jnp.power(jnp.float32(ROPE_THETA), -jnp.arange(0, rot, 2, dtype=F32) / rot)
        ang = pos * inv_freq[None, :]
        cos = jnp.take(jnp.cos(ang), lane % half, axis=1)
        sin = jnp.take(jnp.sin(ang), lane % half, axis=1)
        first, second = (lane < half)[None, :], ((lane >= half) & (lane < rot))[None, :]
        c = jnp.where(first | second, cos, 1.0)
        s_prev = jnp.where(second, sin, 0.0)
        s_next = jnp.where(first, -sin, 0.0)
        return c, s_prev, s_next

    return tables(DIFF_ROT_DIM) + tables(MLA_ROPE)


def _pack_layer(l, w_in, w_q_up, w_kv_up, q_norm_g, kv_norm_g, w_branch, w_out, ssm_glu_w, w_xq, w_xkv, w_xo):
    wi = w_in[l]
    parts = [wi[:, 0:3072], wi[:, 3072:3584], wi[:, 3584:3840], wi[:, 3904:4928], wi[:, 3840:3904],
             jnp.zeros((D_MODEL, 64), F32)]
    wa = jnp.concatenate(parts, axis=1).astype(BF16)
    wqup = w_q_up[l].reshape(MLA_Q_RANK, MLA_HEADS, MLA_NOPE + MLA_ROPE)
    wqup = jnp.pad(wqup, ((0, 0), (0, 0), (0, MLA_QK_PAD - MLA_NOPE - MLA_ROPE)))
    wqup = wqup.reshape(MLA_Q_RANK, MLA_HEADS * MLA_QK_PAD).astype(BF16)
    wkv = w_kv_up[l].reshape(MLA_KV_RANK, MLA_HEADS, MLA_NOPE + MLA_V)
    wkn = wkv[:, :, :MLA_NOPE].reshape(MLA_KV_RANK, MLA_HEADS * MLA_NOPE).astype(BF16)
    wv = wkv[:, :, MLA_NOPE:].reshape(MLA_KV_RANK, MLA_HEADS * MLA_V).astype(BF16)
    wgate = wi[:, 4928:].reshape(D_MODEL, N_BRANCH, D_MODEL).transpose(1, 0, 2).astype(BF16)
    return dict(wa=wa, wqup=wqup, wkn=wkn, wv=wv,
                qng=q_norm_g[l][None, :], kvng=kv_norm_g[l][None, :],
                wgate=wgate, wb=w_branch[l].astype(BF16), wo=w_out[l].astype(BF16),
                wl=ssm_glu_w[l][:, :SSM_CH].astype(BF16), wgl=ssm_glu_w[l][:, SSM_CH:].astype(BF16),
                wxq=w_xq[l].astype(BF16), wxkv=w_xkv[l].astype(BF16), wxo=w_xo[l].astype(BF16))


def _forward(x, mem, positions, w_in, w_q_up, w_kv_up, q_norm_g, kv_norm_g, diff_lambda, diff_subln_g,
             ssm_lam_re, ssm_lam_im, ssm_b_re, ssm_b_im, ssm_c_re, ssm_c_im, ssm_d, ssm_log_dt, ssm_glu_w,
             w_branch, w_out, ln_mix_g, ln_mix_b, w_xq, w_xkv, w_xo, ln_x_g, ln_x_b, ffn_w_gate_up,
             ffn_w_down, moe_router, moe_w_gate_up, moe_w_down, ln_ffn_g, ln_ffn_b, *, ssm_tc, attn_tq):
    s = x.shape[0]
    tabs = _rope_tables(positions)
    row = lambda v: v[None, :]
    ones_g = jnp.ones((1, 128), F32)
    moe_tm = min(512, s // 4)
    for l in range(DEPTH):
        pk = _pack_layer(l, w_in, w_q_up, w_kv_up, q_norm_g, kv_norm_g, w_branch, w_out, ssm_glu_w,
                         w_xq, w_xkv, w_xo)
        qd, kd, vd, qm, km, vm, su = _inproj(x, pk['wa'], pk['wqup'], pk['wkn'], pk['wv'], pk['qng'],
                                             pk['kvng'], tabs)
        lam_init = 0.8 - 0.6 * math.exp(-0.3 * l)
        lf = diff_lambda[l]
        lam = (jnp.exp(jnp.sum(lf[0] * lf[1])) - jnp.exp(jnp.sum(lf[2] * lf[3])) + lam_init).reshape(1)
        o_diff = _attention(qd, kd, vd, lam, row(diff_subln_g[l]), heads=DIFF_HEADS, dqk=128, diff=True,
                            post_scale=1.0 - lam_init, tq=min(2 * attn_tq, s), tk=min(2 * attn_tq, s))
        o_mla = _attention(qm, km, vm, lam, ones_g, heads=MLA_HEADS, dqk=MLA_QK_PAD, diff=False,
                           tq=min(4 * attn_tq, s), tk=min(2 * attn_tq, s))
        sp = _ssm_params(ssm_lam_re[l], ssm_lam_im[l], ssm_b_re[l], ssm_b_im[l], ssm_c_re[l], ssm_c_im[l],
                         ssm_log_dt[l], s // SSM_SEGS)
        y = _ssm(su, sp, row(ssm_d[l]), ssm_tc)
        o_ssm = _glu(y, pk['wl'], pk['wgl'])
        x = _merge(x, o_diff, o_mla, o_ssm, pk['wgate'], pk['wb'], pk['wo'], row(ln_mix_g[l]), row(ln_mix_b[l]))
        kv = _memkv(mem, pk['wxkv'])
        x = _xattn(x, pk['wxq'], kv, pk['wxo'], row(ln_x_g[l]), row(ln_x_b[l]))
        if l % 2 == 0:
            x = _ffn(x, ffn_w_gate_up[l // 2].astype(BF16), ffn_w_down[l // 2].astype(BF16),
                     row(ln_ffn_g[l]), row(ln_ffn_b[l]))
        else:
            wr = jnp.pad(moe_router[l // 2], ((0, 0), (0, 128 - N_EXPERTS)))
            wts, idx = _router(x, wr)
            src, dest, tile_expert, n_valid = _moe_plan(idx[:, :2], moe_tm)
            y_sorted = _moe_ffn(x, src, tile_expert, n_valid, moe_w_gate_up[l // 2].astype(BF16),
                                moe_w_down[l // 2].astype(BF16), moe_tm)
            x = _moe_combine(x, wts, y_sorted, dest, row(ln_ffn_g[l]), row(ln_ffn_b[l]))
    return x


def kernel(x, mem, positions, w_in, w_q_up, w_kv_up, q_norm_g, kv_norm_g, diff_lambda, diff_subln_g, ssm_lam_re, ssm_lam_im, ssm_b_re, ssm_b_im, ssm_c_re, ssm_c_im, ssm_d, ssm_log_dt, ssm_glu_w, w_branch, w_out, ln_mix_g, ln_mix_b, w_xq, w_xkv, w_xo, ln_x_g, ln_x_b, ffn_w_gate_up, ffn_w_down, moe_router, moe_w_gate_up, moe_w_down, ln_ffn_g, ln_ffn_b):
    assert x.shape[0] == 1 and mem.shape[0] == 1
    seq = x.shape[1]
    out = _forward(x[0], mem[0], positions[0], w_in, w_q_up, w_kv_up, q_norm_g, kv_norm_g, diff_lambda,
                   diff_subln_g, ssm_lam_re, ssm_lam_im, ssm_b_re, ssm_b_im, ssm_c_re, ssm_c_im, ssm_d,
                   ssm_log_dt, ssm_glu_w, w_branch, w_out, ln_mix_g, ln_mix_b, w_xq, w_xkv, w_xo, ln_x_g,
                   ln_x_b, ffn_w_gate_up, ffn_w_down, moe_router, moe_w_gate_up, moe_w_down, ln_ffn_g,
                   ln_ffn_b, ssm_tc=min(256, seq // SSM_SEGS), attn_tq=min(512, seq))
    return out[None]
```

```python
import functools
import math

import jax
import jax.numpy as jnp
from jax import lax
from jax.experimental import pallas as pl
from jax.experimental.pallas import tpu as pltpu

D_MODEL = 2048
DEPTH = 2
MEM_LEN = 256
DIFF_HEADS = 8
DIFF_HEAD_DIM = 64
DIFF_V_DIM = 128
DIFF_ROT_DIM = 16
DIFF_QK_W = 1024
DIFF_V_W = 1024
MLA_HEADS = 8
MLA_Q_RANK = 512
MLA_KV_RANK = 256
MLA_NOPE = 128
MLA_ROPE = 64
MLA_V = 128
MLA_QK_PAD = 256
V_ROWS = 144
ATTN_SLOTS = 3
SSM_CH = 1024
SSM_GROUP = 16
SSM_GROUPS = 64
SSM_STATE = 64
SSM_SEGS = 8
SSM_BLOCKS = 8
SSM_BLOCK_STATE = 512
N_BRANCH = 3
BRANCH_W = 1024
X_HEADS = 4
X_HEAD_DIM = 128
X_W = 512
D_FF = 5632
N_EXPERTS = 8
ROPE_THETA = 500000.0
LN_EPS = 1e-5
RMS_EPS = 1e-6
NEG_BIG = -1e30
ALPHA = (2 * DEPTH) ** 0.25
LOG2E = math.log2(math.e)

A_DQ, A_DK, A_DV, A_MQ, A_MKV, A_SU, A_MKR, A_END = 0, 1024, 2048, 3072, 3584, 3840, 4864, 4992

VMEM_LIMIT = 56 * 1024 * 1024
F32 = jnp.float32
BF16 = jnp.bfloat16
HI = lax.Precision.HIGHEST


def _cparams(sem):
    return pltpu.CompilerParams(dimension_semantics=sem, vmem_limit_bytes=VMEM_LIMIT)


def _layer_norm(z, g, b):
    mu = jnp.mean(z, axis=-1, keepdims=True)
    zc = z - mu
    var = jnp.mean(zc * zc, axis=-1, keepdims=True)
    return zc * lax.rsqrt(var + LN_EPS) * g + b


def _rms_norm(z, g):
    return z * lax.rsqrt(jnp.mean(z * z, axis=-1, keepdims=True) + RMS_EPS) * g


def _dot(a, b):
    return jnp.dot(a, b, preferred_element_type=F32)


def _dot_nt(a, b):
    return lax.dot_general(a, b, (((1,), (1,)), ((), ())), preferred_element_type=F32)


def _inproj_kernel(x_ref, wa_ref, wqup_ref, wkn_ref, wv_ref, qng_ref, kvng_ref,
                   cd_ref, sda_ref, sdb_ref, cm_ref, sma_ref, smb_ref,
                   qd_ref, kd_ref, vd_ref, qm_ref, km_ref, vm_ref, su_ref):
    xb = x_ref[...].astype(BF16)
    h = _dot(xb, wa_ref[...])
    cd, sda, sdb = cd_ref[...], sda_ref[...], sdb_ref[...]
    cm, sma, smb = cm_ref[...], sma_ref[...], smb_ref[...]

    def rope_d(blk):
        return blk * cd + pltpu.roll(blk, 8, 1) * sda + pltpu.roll(blk, 120, 1) * sdb

    def rope_m(blk):
        return blk * cm + pltpu.roll(blk, 32, 1) * sma + pltpu.roll(blk, 96, 1) * smb

    q_scale = DIFF_HEAD_DIM ** -0.5 * LOG2E
    for c in range(DIFF_QK_W // 128):
        lo, hi = c * 128, (c + 1) * 128
        qd_ref[:, lo:hi] = (rope_d(h[:, A_DQ + lo:A_DQ + hi]) * q_scale).astype(BF16)
        kd_ref[:, lo:hi] = rope_d(h[:, A_DK + lo:A_DK + hi]).astype(BF16)
    vd_ref[...] = h[:, A_DV:A_MQ].astype(BF16)
    su_ref[...] = h[:, A_SU:A_MKR]

    cq = _rms_norm(h[:, A_MQ:A_MKV], qng_ref[...]).astype(BF16)
    qm = _dot(cq, wqup_ref[...]) * ((MLA_NOPE + MLA_ROPE) ** -0.5 * LOG2E)
    ckv = _rms_norm(h[:, A_MKV:A_SU], kvng_ref[...]).astype(BF16)
    kn = _dot(ckv, wkn_ref[...])
    vm_ref[...] = _dot(ckv, wv_ref[...]).astype(BF16)
    krot = rope_m(h[:, A_MKR:A_END]).astype(BF16)
    for hh in range(MLA_HEADS):
        b0 = hh * MLA_QK_PAD
        qm_ref[:, b0:b0 + 128] = qm[:, b0:b0 + 128].astype(BF16)
        qm_ref[:, b0 + 128:b0 + 256] = rope_m(qm[:, b0 + 128:b0 + 256]).astype(BF16)
        km_ref[:, b0:b0 + 128] = kn[:, hh * 128:(hh + 1) * 128].astype(BF16)
        km_ref[:, b0 + 128:b0 + 256] = krot


def _inproj(x, wa, wqup, wkn, wv, qng, kvng, tabs, tm=256):
    s = x.shape[0]
    row = lambda w: pl.BlockSpec((tm, w), lambda i: (i, 0))
    const = lambda a: pl.BlockSpec(a.shape, lambda i: (0,) * a.ndim, pipeline_mode=pl.Buffered(1))
    out_shapes = [jax.ShapeDtypeStruct((s, 1024), BF16), jax.ShapeDtypeStruct((s, 1024), BF16),
                  jax.ShapeDtypeStruct((s, 1024), BF16), jax.ShapeDtypeStruct((s, 2048), BF16),
                  jax.ShapeDtypeStruct((s, 2048), BF16), jax.ShapeDtypeStruct((s, 1024), BF16),
                  jax.ShapeDtypeStruct((s // SSM_SEGS, SSM_SEGS * SSM_CH), F32)]
    seg_tiles = s // SSM_SEGS // tm
    su_spec = pl.BlockSpec((tm, SSM_CH), lambda i: (i % seg_tiles, i // seg_tiles))
    return pl.pallas_call(
        _inproj_kernel,
        grid=(s // tm,),
        in_specs=[row(D_MODEL), const(wa), const(wqup), const(wkn), const(wv), const(qng), const(kvng)]
                 + [row(128)] * 6,
        out_specs=[row(1024), row(1024), row(1024), row(2048), row(2048), row(1024), su_spec],
        out_shape=out_shapes,
        compiler_params=_cparams(("parallel",)),
        name="inproj",
    )(x, wa, wqup, wkn, wv, qng, kvng, *tabs)


def _col_reduce(op, s, ways=8):
    parts = s.reshape(ways, s.shape[0] // ways, s.shape[1])
    return op(op(parts, axis=0), axis=0, keepdims=True)


def _attn_kernel(lam_ref, q_ref, k_ref, vt_ref, g_ref, o_ref, q2_sc, m_sc, acc_sc, s_sc, *,
                 tq, tk, cw, diff, post_scale):
    qi = pl.program_id(1)
    qt = q_ref[...].astype(F32).T.astype(BF16)
    if diff:
        feat = lax.broadcasted_iota(jnp.int32, qt.shape, 0)
        zero = jnp.zeros_like(qt)
        q2_sc[:, :tq] = jnp.where(feat < DIFF_HEAD_DIM, qt, zero)
        q2_sc[:, tq:] = jnp.where(feat >= DIFF_HEAD_DIM, qt, zero)
    else:
        q2_sc[...] = qt
    m_sc[...] = jnp.full(m_sc.shape, NEG_BIG, F32)
    acc_sc[...] = jnp.zeros(acc_sc.shape, F32)
    ratio = tq // tk
    cols = q2_sc.shape[1]

    def step(j, diag):
        q0 = lambda c: (c * cw) % tq
        if diag is None:
            units = [(0, tk, c) for c in range(cols // cw)]
            k_full = k_ref[pl.ds(pl.multiple_of(j * tk, tk), tk), :]
            v_full = vt_ref[j]
            keys = lambda ks, kl: k_full
            vals = lambda ks, kl: v_full
        else:
            kl = min(cw, tk)
            units = [(ks, kl, c) for ks in range(0, tk, kl) for c in range(cols // cw)
                     if diag * tk + ks < q0(c) + cw]
            keys = lambda ks, kl: k_ref[pl.ds(pl.multiple_of(j * tk + ks, kl), kl), :]
            vals = lambda ks, kl: vt_ref[j, :, ks:ks + kl]
        nu = len(units)
        cs = lambda c: slice(c * cw, (c + 1) * cw)
        slots = s_sc.shape[0]

        def qk(n):
            ks, kl, c = units[n]
            s = _dot(keys(ks, kl), q2_sc[:, cs(c)])
            if diag is not None:
                kpos = lax.broadcasted_iota(jnp.int32, s.shape, 0) + (diag * tk + ks)
                qpos = lax.broadcasted_iota(jnp.int32, s.shape, 1) + q0(c)
                s = jnp.where(qpos >= kpos, s, NEG_BIG)
            s_sc[n % slots, :kl, :] = s
            return _col_reduce(jnp.max, s)

        def stats(n):
            c = units[n][2]
            m_old = m_sc[:, cs(c)]
            m_new = jnp.maximum(m_old, raw.pop(n))
            m_sc[:, cs(c)] = m_new
            return m_new, jnp.exp2(m_old - m_new)

        def finish(n, m_new, a):
            ks, kl, c = units[n]
            p = jnp.exp2((s_sc[n % slots, :kl, :] - m_new).astype(BF16))
            acc_sc[:, cs(c)] = a * acc_sc[:, cs(c)] + _dot(vals(ks, kl), p)

        ahead = slots - 1
        raw = {n: qk(n) for n in range(min(ahead, nu))}
        st = stats(0)
        for n in range(nu):
            if n + ahead < nu:
                raw[n + ahead] = qk(n + ahead)
            st_next = stats(n + 1) if n + 1 < nu else None
            finish(n, *st)
            st = st_next

    def body(j, carry):
        step(j, None)
        return carry

    lax.fori_loop(0, qi * ratio, body, 0)
    for d in range(ratio):
        step(qi * ratio + d, d)
    ot = acc_sc[:128, :] / acc_sc[128:129, :]
    if diff:
        ot = ot[:, :tq] - lam_ref[0] * ot[:, tq:]
    o = ot.T
    if diff:
        o = _rms_norm(o, g_ref[...]) * post_scale
    o_ref[...] = o.astype(o_ref.dtype)


def _attention(q, k, v, lam, g, *, heads, dqk, diff, post_scale=1.0, tq=512, tk=512, cw=512):
    s = q.shape[0]
    cols = 2 * tq if diff else tq
    nch = s // tk
    vt =v.reshape(nch, tk, heads, 128).transpose(2, 0, 3, 1)
    ones = jnp.ones((heads, nch, V_ROWS - 128, tk), BF16)
    vt = jnp.concatenate([vt, ones], axis=2).reshape(heads * nch, V_ROWS, tk)
    kern = functools.partial(_attn_kernel, tq=tq, tk=tk, cw=cw, diff=diff, post_scale=post_scale)
    return pl.pallas_call(
        kern,
        grid=(heads, s // tq),
        in_specs=[pl.BlockSpec(memory_space=pltpu.SMEM),
                  pl.BlockSpec((tq, dqk), lambda h, i: (i, h)),
                  pl.BlockSpec((s, dqk), lambda h, i: (0, h)),
                  pl.BlockSpec((nch, V_ROWS, tk), lambda h, i: (h, 0, 0)),
                  pl.BlockSpec((1, 128), lambda h, i: (0, 0))],
        out_specs=pl.BlockSpec((tq, 128), lambda h, i: (i, h)),
        out_shape=jax.ShapeDtypeStruct((s, heads * 128), BF16),
        scratch_shapes=[pltpu.VMEM((dqk, cols), BF16), pltpu.VMEM((1, cols), F32),
                        pltpu.VMEM((V_ROWS, cols), F32), pltpu.VMEM((ATTN_SLOTS, tk, cw), F32)],
        compiler_params=_cparams(("parallel", "parallel")),
        name="diff_attn" if diff else "mla_attn",
    )(lam, q, k, vt, g)


def _ssm_kernel(u_ref, wb_ref, ar_ref, ai_ref, wc_ref, d_ref, x0_ref, *refs, tc, emit_y):
    if emit_y:
        y_ref, x_sc, s_sc = refs
    else:
        xe_ref, x_sc, s_sc = refs
    i = pl.program_id(1)
    ns = SSM_BLOCK_STATE

    @pl.when(i == 0)
    def _():
        s_sc[...] = x0_ref[0]

    u = u_ref[...]
    x_sc[...] = _dot(u.astype(BF16), wb_ref[0])
    ar = jnp.broadcast_to(ar_ref[0], (SSM_SEGS, ns))
    ai = jnp.broadcast_to(ai_ref[0], (SSM_SEGS, ns))

    def body(t, carry):
        re, im = carry
        rows = pl.ds(pl.multiple_of(t * SSM_SEGS, SSM_SEGS), SSM_SEGS)
        nre = ar * re - ai * im + x_sc[rows, :ns]
        nim = ar * im + ai * re + x_sc[rows, ns:]
        x_sc[rows, :ns] = nre
        x_sc[rows, ns:] = nim
        return nre, nim

    re, im = lax.fori_loop(0, tc, body, (s_sc[:, :ns], s_sc[:, ns:]), unroll=8)
    s_sc[:, :ns] = re
    s_sc[:, ns:] = im
    if emit_y:
        y = _dot(x_sc[...].astype(BF16), wc_ref[0]) + d_ref[...] * u
        y_ref[...] = jax.nn.gelu(y).astype(y_ref.dtype)
    else:
        @pl.when(i == pl.num_programs(1) - 1)
        def _():
            xe_ref[0] = s_sc[...]


def _ssm_pass(u_perm, p, d, x0, tc, emit_y):
    rows = u_perm.shape[0]
    nchunk = rows // (tc * SSM_SEGS)
    blk3 = lambda a: pl.BlockSpec((1,) + a.shape[1:], lambda j, i: (j, 0, 0))
    if emit_y:
        out_specs = pl.BlockSpec((tc * SSM_SEGS, 128), lambda j, i: (i, j))
        out_shape = jax.ShapeDtypeStruct((rows, SSM_CH), BF16)
    else:
        out_specs = blk3(x0)
        out_shape = jax.ShapeDtypeStruct(x0.shape, F32)
    return pl.pallas_call(
        functools.partial(_ssm_kernel, tc=tc, emit_y=emit_y),
        grid=(SSM_BLOCKS, nchunk),
        in_specs=[pl.BlockSpec((tc * SSM_SEGS, 128), lambda j, i: (i, j)),
                  blk3(p['wb']), blk3(p['ar']), blk3(p['ai']), blk3(p['wc']),
                  pl.BlockSpec((1, 128), lambda j, i: (0, j)), blk3(x0)],
        out_specs=out_specs,
        out_shape=out_shape,
        scratch_shapes=[pltpu.VMEM((tc * SSM_SEGS, 2 * SSM_BLOCK_STATE), F32),
                        pltpu.VMEM((SSM_SEGS, 2 * SSM_BLOCK_STATE), F32)],
        compiler_params=_cparams(("parallel", "arbitrary")),
        name="ssm_scan" if emit_y else "ssm_state",
    )(u_perm, p['wb'], p['ar'], p['ai'], p['wc'], d, x0)


def _ssm_params(lam_re, lam_im, b_re, b_im, c_re, c_im, log_dt, seg_len):
    dt = jnp.exp(log_dt)[:, None]

    def cexp(scale):
        mag = jnp.exp(lam_re * scale)
        return mag * jnp.cos(lam_im * scale), mag * jnp.sin(lam_im * scale)

    ar, ai = cexp(dt)
    sr, si = cexp(dt * seg_len)
    den = lam_re * lam_re + lam_im * lam_im
    fr = ((ar - 1.0) * lam_re + ai * lam_im) / den
    fi = (ai * lam_re - (ar - 1.0) * lam_im) / den
    bbr = fr[..., None] * b_re - fi[..., None] * b_im
    bbi = fr[..., None] * b_im + fi[..., None] * b_re
    eye = jnp.eye(SSM_BLOCKS, dtype=F32)

    def in_mat(w):
        w = w.reshape(SSM_BLOCKS, 8, SSM_STATE, SSM_GROUP)
        return jnp.einsum('jgnp,gh->jgphn', w, eye).reshape(SSM_BLOCKS, 8 * SSM_GROUP, 8 * SSM_STATE)

    def out_mat(w):
        w = w.reshape(SSM_BLOCKS, 8, SSM_GROUP, SSM_STATE)
        return jnp.einsum('jgpn,gh->jgnhp', w, eye).reshape(SSM_BLOCKS, 8 * SSM_STATE, 8 * SSM_GROUP)

    vec = lambda a: a.reshape(SSM_BLOCKS, 1, SSM_BLOCK_STATE)
    return dict(wb=jnp.concatenate([in_mat(bbr), in_mat(bbi)], axis=2).astype(BF16),
                wc=jnp.concatenate([out_mat(c_re), -out_mat(c_im)], axis=1).astype(BF16),
                ar=vec(ar), ai=vec(ai), sr=vec(sr), si=vec(si))


def _ssm(su_seg, p, d, tc):
    seg_len = su_seg.shape[0]
    s = seg_len * SSM_SEGS
    u_perm = su_seg.reshape(s, SSM_CH)
    ns = SSM_BLOCK_STATE
    zeros = jnp.zeros((SSM_BLOCKS, SSM_SEGS, 2 * ns), F32)
    xe = _ssm_pass(u_perm, p, d, zeros, tc, emit_y=False)
    x0r, x0i = [zeros[:, 0, :ns]], [zeros[:, 0, ns:]]
    for sg in range(SSM_SEGS - 1):
        pr, pi = x0r[-1], x0i[-1]
        x0r.append(p['sr'][:, 0] * pr - p['si'][:, 0] * pi + xe[:, sg, :ns])
        x0i.append(p['sr'][:, 0] * pi + p['si'][:, 0] * pr + xe[:, sg, ns:])
    x0 = jnp.concatenate([jnp.stack(x0r, axis=1), jnp.stack(x0i, axis=1)], axis=2)
    y = _ssm_pass(u_perm, p, d, x0, tc, emit_y=True)
    return y.reshape(seg_len, SSM_SEGS, SSM_CH).transpose(1, 0, 2).reshape(s, SSM_CH)


def _glu_kernel(y_ref, wl_ref, wg_ref, o_ref):
    y = y_ref[...]
    o_ref[...] = (_dot(y, wl_ref[...]) * jax.nn.sigmoid(_dot(y, wg_ref[...]))).astype(o_ref.dtype)


def _glu(y, wl, wg, tm=512):
    s = y.shape[0]
    return pl.pallas_call(
        _glu_kernel,
        grid=(s // tm,),
        in_specs=[pl.BlockSpec((tm, SSM_CH), lambda i: (i, 0)),
                  pl.BlockSpec(wl.shape, lambda i: (0, 0)),
                  pl.BlockSpec(wg.shape, lambda i: (0, 0))],
        out_specs=pl.BlockSpec((tm, SSM_CH), lambda i: (i, 0)),
        out_shape=jax.ShapeDtypeStruct((s, SSM_CH), BF16),
        compiler_params=_cparams(("parallel",)),
        name="ssm_glu",
    )(y, wl, wg)


def _merge_kernel(x_ref, od_ref, om_ref, os_ref, wg_ref, wb_ref, wo_ref, g_ref, b_ref, o_ref, xb_sc):
    j = pl.program_id(1)

    @pl.when(j == 0)
    def _():
        xb_sc[...] = x_ref[...].astype(BF16)
        o_ref[...] = jnp.zeros(o_ref.shape, F32)

    xb = xb_sc[...]
    merged = None
    for n, br_ref in enumerate((od_ref, om_ref, os_ref)):
        term = jax.nn.sigmoid(_dot(xb, wg_ref[n])) * _dot(br_ref[...], wb_ref[n])
        merged = term if merged is None else merged + term
    o_ref[...] += _dot(merged.astype(BF16), wo_ref[...])

    @pl.when(j == pl.num_programs(1) - 1)
    def _():
        o_ref[...] = _layer_norm(ALPHA * x_ref[...] + o_ref[...], g_ref[...], b_ref[...])


def _merge(x, od, om, os_, wg, wb, wo, g, b, tm=512, tn=512):
    s = x.shape[0]
    br = pl.BlockSpec((tm, BRANCH_W), lambda i, j: (i, 0))
    vec = pl.BlockSpec((1, D_MODEL), lambda i, j: (0, 0))
    return pl.pallas_call(
        _merge_kernel,
        grid=(s // tm, D_MODEL // tn),
        in_specs=[pl.BlockSpec((tm, D_MODEL), lambda i, j: (i, 0)), br, br, br,
                  pl.BlockSpec((N_BRANCH, D_MODEL, tn), lambda i, j: (0, 0, j)),
                  pl.BlockSpec((N_BRANCH, BRANCH_W, tn), lambda i, j: (0, 0, j)),
                  pl.BlockSpec((tn, D_MODEL), lambda i, j: (j, 0)), vec, vec],
        out_specs=pl.BlockSpec((tm, D_MODEL), lambda i, j: (i, 0)),
        out_shape=jax.ShapeDtypeStruct((s, D_MODEL), F32),
        scratch_shapes=[pltpu.VMEM((tm, D_MODEL), BF16)],
        compiler_params=_cparams(("parallel", "arbitrary")),
        name="merge_out_ln",
    )(x, od, om, os_, wg, wb, wo, g, b)


def _memkv_kernel(m_ref, w_ref, o_ref):
    o_ref[...] = _dot(m_ref[...].astype(BF16), w_ref[...]).astype(o_ref.dtype)


def _memkv(mem, w):
    return pl.pallas_call(
        _memkv_kernel,
        out_shape=jax.ShapeDtypeStruct((mem.shape[0], w.shape[1]), BF16),
        compiler_params=pltpu.CompilerParams(vmem_limit_bytes=VMEM_LIMIT),
        name="mem_kv",
    )(mem, w)


def _xattn_kernel(x_ref, wq_ref, kv_ref, wo_ref, g_ref, b_ref, o_ref):
    x = x_ref[...]
    q = (_dot(x.astype(BF16), wq_ref[...]) * (X_HEAD_DIM ** -0.5)).astype(BF16)
    outs = []
    for h in range(X_HEADS):
        lo, hi = h * X_HEAD_DIM, (h + 1) * X_HEAD_DIM
        s = _dot_nt(q[:, lo:hi], kv_ref[:, lo:hi])
        p = jnp.exp(s - jnp.max(s, axis=1, keepdims=True))
        o = _dot(p.astype(BF16), kv_ref[:, X_W + lo:X_W + hi])
        outs.append((o / jnp.sum(p, axis=1, keepdims=True)).astype(BF16))
    xa = _dot(jnp.concatenate(outs, axis=1), wo_ref[...])
    o_ref[...] = _layer_norm(ALPHA * x + xa, g_ref[...], b_ref[...])


def _xattn(x, wq, kv, wo, g, b, tm=512):
    s = x.shape[0]
    full = lambda a: pl.BlockSpec(a.shape, lambda i: (0, 0))
    return pl.pallas_call(
        _xattn_kernel,
        grid=(s // tm,),
        in_specs=[pl.BlockSpec((tm, D_MODEL), lambda i: (i, 0)), full(wq), full(kv), full(wo), full(g), full(b)],
        out_specs=pl.BlockSpec((tm, D_MODEL), lambda i: (i, 0)),
        out_shape=jax.ShapeDtypeStruct((s, D_MODEL), F32),
        compiler_params=_cparams(("parallel",)),
        name="xattn_ln",
    )(x, wq, kv, wo, g, b)


def _ffn_kernel(x_ref, wg_ref, wu_ref, wd_ref, g_ref, b_ref, o_ref, xb_sc):
    f = pl.program_id(1)

    @pl.when(f == 0)
    def _():
        xb_sc[...] = x_ref[...].astype(BF16)
        o_ref[...] = jnp.zeros(o_ref.shape, F32)

    xb = xb_sc[...]
    a = jax.nn.silu(_dot(xb, wg_ref[...])) * _dot(xb, wu_ref[...])
    o_ref[...] += _dot(a.astype(BF16), wd_ref[...])

    @pl.when(f == pl.num_programs(1) - 1)
    def _():
        o_ref[...] = _layer_norm(ALPHA * x_ref[...] + o_ref[...], g_ref[...], b_ref[...])


def _ffn(x, wgu, wd, g, b, tm=512, tf=512):
    s = x.shape[0]
    nf = D_FF // tf
    vec = pl.BlockSpec((1, D_MODEL), lambda i, f: (0, 0))
    return pl.pallas_call(
        _ffn_kernel,
        grid=(s // tm, nf),
        in_specs=[pl.BlockSpec((tm, D_MODEL), lambda i, f: (i, 0)),
                  pl.BlockSpec((D_MODEL, tf), lambda i, f: (0, f)),
                  pl.BlockSpec((D_MODEL, tf), lambda i, f: (0, nf + f)),
                  pl.BlockSpec((tf, D_MODEL), lambda i, f: (f, 0)), vec, vec],
        out_specs=pl.BlockSpec((tm, D_MODEL), lambda i, f: (i, 0)),
        out_shape=jax.ShapeDtypeStruct((s, D_MODEL), F32),
        scratch_shapes=[pltpu.VMEM((tm, D_MODEL), BF16)],
        compiler_params=_cparams(("parallel", "arbitrary")),
        name="ffn_ln",
    )(x, wgu, wgu, wd, g, b)


def _router_kernel(x_ref, w_ref, wt_ref, ix_ref):
    logits = jnp.dot(x_ref[...], w_ref[...], precision=HI, preferred_element_type=F32)
    lane = lax.broadcasted_iota(jnp.int32, logits.shape, 1)
    logits = jnp.where(lane < N_EXPERTS, logits, NEG_BIG)
    v1 = jnp.max(logits, axis=1, keepdims=True)
    i1 = jnp.min(jnp.where(logits == v1, lane, 128), axis=1, keepdims=True)
    rest = jnp.where(lane == i1, NEG_BIG, logits)
    v2 = jnp.max(rest, axis=1, keepdims=True)
    i2 = jnp.min(jnp.where(rest == v2, lane, 128), axis=1, keepdims=True)
    e2 = jnp.exp(v2 - v1)
    w1 = 1.0 / (1.0 + e2)
    w2 = e2 / (1.0 + e2)
    wt_ref[...] = jnp.where(lane == 0, w1, jnp.where(lane == 1, w2, 0.0))
    ix_ref[...] = jnp.where(lane == 0, i1, jnp.where(lane == 1, i2, 0))


def _router(x, w_pad, tm=512):
    s = x.shape[0]
    out = pl.BlockSpec((tm, 128), lambda i: (i, 0))
    return pl.pallas_call(
        _router_kernel,
        grid=(s // tm,),
        in_specs=[pl.BlockSpec((tm, D_MODEL), lambda i: (i, 0)), pl.BlockSpec(w_pad.shape, lambda i: (0, 0))],
        out_specs=[out, out],
        out_shape=[jax.ShapeDtypeStruct((s, 128), F32), jax.ShapeDtypeStruct((s, 128), jnp.int32)],
        compiler_params=_cparams(("parallel",)),
        name="router_top2",
    )(x, w_pad)


def _moe_plan(expert_idx, tm):
    n_pairs = expert_idx.shape[0] * 2
    n_tiles = n_pairs // tm + N_EXPERTS
    ef = expert_idx.reshape(-1)
    onehot = (ef[:, None] == jnp.arange(N_EXPERTS)[None, :]).astype(jnp.int32)
    rank = jnp.take_along_axis(jnp.cumsum(onehot, axis=0) - onehot, ef[:, None], axis=1)[:, 0]
    padded = (jnp.sum(onehot, axis=0) + tm - 1) // tm * tm
    ends = jnp.cumsum(padded)
    dest = (ends - padded)[ef] + rank
    src = jnp.zeros((n_tiles * tm,), jnp.int32).at[dest].set(jnp.arange(n_pairs, dtype=jnp.int32) // 2)
    n_valid = ends[-1] // tm
    tile_start = jnp.minimum(jnp.arange(n_tiles), n_valid - 1) * tm
    tile_expert = jnp.sum((tile_start[:, None] >= ends[None, :]).astype(jnp.int32), axis=1)
    return src.reshape(n_tiles, 1, tm), dest, tile_expert.astype(jnp.int32), n_valid.reshape(1).astype(jnp.int32)


def _row_copy(src_hbm, row, dst_vmem, r, sem):
    return pltpu.make_async_copy(src_hbm.at[pl.ds(row, 1), :], dst_vmem.at[pl.ds(r, 1), :], sem)


def _gather_start(src_hbm, row_of, dst_vmem, sem, n):
    def start(r, carry):
        _row_copy(src_hbm, row_of(r), dst_vmem, r, sem).start()
        return carry

    lax.fori_loop(0, n, start, 0, unroll=8)


def _gather_wait(src_hbm, dst_vmem, sem, n):
    def wait(r, carry):
        _row_copy(src_hbm, 0, dst_vmem, r, sem).wait()
        return carry

    lax.fori_loop(0, n, wait, 0, unroll=8)


def _moe_ffn_kernel(te_ref, nv_ref, src_ref, nxt_ref, x_hbm, wg_ref, wu_ref, wd_ref, o_ref,
                    xg_sc, xb_sc, sem, *, tm):
    i, f = pl.program_id(0), pl.program_id(1)
    n_valid = nv_ref[0]
    valid = i < n_valid
    slot = i % 2

    @pl.when(f == 0)
    def _():
        o_ref[...] = jnp.zeros(o_ref.shape, F32)

    @pl.when((i == 0) & (f == 0))
    def _():
        _gather_start(x_hbm, lambda r: src_ref[0, 0, r], xg_sc.at[0], sem.at[0], tm)

    @pl.when(valid & (f == 0))
    def _():
        _gather_wait(x_hbm, xg_sc.at[slot], sem.at[slot], tm)

        @pl.when(i + 1 < n_valid)
        def _():
            _gather_start(x_hbm, lambda r: nxt_ref[0, 0, r], xg_sc.at[1 - slot], sem.at[1 - slot], tm)

        xb_sc[...] = xg_sc[slot].astype(BF16)

    @pl.when(valid)
    def _():
        xb = xb_sc[...]
        a = jax.nn.silu(_dot(xb, wg_ref[0])) * _dot(xb, wu_ref[0])
        o_ref[...] += _dot(a.astype(BF16), wd_ref[0])


def _moe_ffn(x, src, tile_expert, n_valid, wgu, wd, tm, tf=512):
    n_tiles = src.shape[0]
    nf = D_FF // tf
    fcol = lambda i, f, nv: jnp.where(i < nv[0], f, nf - 1)
    grid_spec = pltpu.PrefetchScalarGridSpec(
        num_scalar_prefetch=2,
        grid=(n_tiles, nf),
        in_specs=[pl.BlockSpec((1, 1, tm), lambda i, f, te, nv: (i, 0, 0), memory_space=pltpu.SMEM),
                  pl.BlockSpec((1, 1, tm), lambda i, f, te, nv: (jnp.minimum(i + 1, n_tiles - 1), 0, 0),
                               memory_space=pltpu.SMEM),
                  pl.BlockSpec(memory_space=pl.ANY),
                  pl.BlockSpec((1, D_MODEL, tf), lambda i, f, te, nv: (te[i], 0, fcol(i, f, nv))),
                  pl.BlockSpec((1, D_MODEL, tf), lambda i, f, te, nv: (te[i], 0, nf + fcol(i, f, nv))),
                  pl.BlockSpec((1, tf, D_MODEL), lambda i, f, te, nv: (te[i], fcol(i, f, nv), 0))],
        out_specs=pl.BlockSpec((tm, D_MODEL), lambda i, f, te, nv: (i, 0)),
        scratch_shapes=[pltpu.VMEM((2, tm, D_MODEL), F32), pltpu.VMEM((tm, D_MODEL), BF16),
                        pltpu.SemaphoreType.DMA((2,))])
    return pl.pallas_call(
        functools.partial(_moe_ffn_kernel, tm=tm),
        grid_spec=grid_spec,
        out_shape=jax.ShapeDtypeStruct((n_tiles * tm, D_MODEL), F32),
        compiler_params=_cparams(("arbitrary", "arbitrary")),
        name="moe_ffn",
    )(tile_expert, n_valid, src, src, x, wgu, wgu, wd)


def _moe_combine_kernel(p1_ref, p2_ref, x_ref, wt_ref, y_hbm, g_ref, b_ref, o_ref, y1_sc, y2_sc, sem, *, tm):
    _gather_start(y_hbm, lambda r: p1_ref[0, 0, r], y1_sc, sem.at[0], tm)
    _gather_start(y_hbm, lambda r: p2_ref[0, 0, r], y2_sc, sem.at[1], tm)
    _gather_wait(y_hbm, y1_sc, sem.at[0], tm)
    _gather_wait(y_hbm, y2_sc, sem.at[1], tm)
    wt = wt_ref[...]
    f = wt[:, 0:1] * y1_sc[...] + wt[:, 1:2] * y2_sc[...]
    o_ref[...] = _layer_norm(ALPHA * x_ref[...] + f, g_ref[...], b_ref[...])


def _moe_combine(x, wts, y_sorted, dest, g, b, tm=256):
    s = x.shape[0]
    pos = dest.reshape(s // tm, tm, 2)
    p1, p2 = pos[:, None, :, 0], pos[:, None, :, 1]
    smem = pl.BlockSpec((1, 1, tm), lambda i: (i, 0, 0), memory_space=pltpu.SMEM)
    vec = pl.BlockSpec((1, D_MODEL), lambda i: (0, 0))
    return pl.pallas_call(
        functools.partial(_moe_combine_kernel, tm=tm),
        grid=(s // tm,),
        in_specs=[smem, smem, pl.BlockSpec((tm, D_MODEL), lambda i: (i, 0)),
                  pl.BlockSpec((tm, 128), lambda i: (i, 0)), pl.BlockSpec(memory_space=pl.ANY), vec, vec],
        out_specs=pl.BlockSpec((tm, D_MODEL), lambda i: (i, 0)),
        out_shape=jax.ShapeDtypeStruct((s, D_MODEL), F32),
        scratch_shapes=[pltpu.VMEM((tm, D_MODEL), F32), pltpu.VMEM((tm, D_MODEL), F32),
                        pltpu.SemaphoreType.DMA((2,))],
        compiler_params=_cparams(("arbitrary",)),
        name="moe_combine_ln",
    )(p1, p2, x, wts, y_sorted, g, b)


def _rope_tables(positions):
    pos = positions.astype(F32)[:, None]
    lane = jnp.arange(128) % 64

    def tables(rot):
        half = rot // 2
        inv_freq = jnp.power(jnp.float32(ROPE_THETA), -jnp.arange(0, rot, 2, dtype=F32) / rot)
        ang = pos * inv_freq[None, :]
        cos = jnp.take(jnp.cos(ang), lane % half, axis=1)
        sin = jnp.take(jnp.sin(ang), lane % half, axis=1)
        first, second = (lane < half)[None, :], ((lane >= half) & (lane < rot))[None, :]
        c = jnp.where(first | second, cos, 1.0)
        s_prev = jnp.where(second, sin, 0.0)
        s_next = jnp.where(first, -sin, 0.0)
        return c, s_prev, s_next

    return tables(DIFF_ROT_DIM) + tables(MLA_ROPE)


def _pack_layer(l, w_in, w_q_up, w_kv_up, q_norm_g, kv_norm_g, w_branch, w_out, ssm_glu_w, w_xq, w_xkv, w_xo):
    wi = w_in[l]
    parts = [wi[:, 0:3072], wi[:, 3072:3584], wi[:, 3584:3840], wi[:, 3904:4928], wi[:, 3840:3904],
             jnp.zeros((D_MODEL, 64), F32)]
    wa = jnp.concatenate(parts, axis=1).astype(BF16)
    wqup = w_q_up[l].reshape(MLA_Q_RANK, MLA_HEADS, MLA_NOPE + MLA_ROPE)
    wqup = jnp.pad(wqup, ((0, 0), (0, 0), (0, MLA_QK_PAD - MLA_NOPE - MLA_ROPE)))
    wqup = wqup.reshape(MLA_Q_RANK, MLA_HEADS * MLA_QK_PAD).astype(BF16)
    wkv = w_kv_up[l].reshape(MLA_KV_RANK, MLA_HEADS, MLA_NOPE + MLA_V)
    wkn = wkv[:, :, :MLA_NOPE].reshape(MLA_KV_RANK, MLA_HEADS * MLA_NOPE).astype(BF16)
    wv = wkv[:, :, MLA_NOPE:].reshape(MLA_KV_RANK, MLA_HEADS * MLA_V).astype(BF16)
    wgate = wi[:, 4928:].reshape(D_MODEL, N_BRANCH, D_MODEL).transpose(1, 0, 2).astype(BF16)
    return dict(wa=wa, wqup=wqup, wkn=wkn, wv=wv,
                qng=q_norm_g[l][None, :], kvng=kv_norm_g[l][None, :],
                wgate=wgate, wb=w_branch[l].astype(BF16), wo=w_out[l].astype(BF16),
                wl=ssm_glu_w[l][:, :SSM_CH].astype(BF16), wgl=ssm_glu_w[l][:, SSM_CH:].astype(BF16),
                wxq=w_xq[l].astype(BF16), wxkv=w_xkv[l].astype(BF16), wxo=w_xo[l].astype(BF16))


def _forward(x, mem, positions, w_in, w_q_up, w_kv_up, q_norm_g, kv_norm_g, diff_lambda, diff_subln_g,
             ssm_lam_re, ssm_lam_im, ssm_b_re, ssm_b_im, ssm_c_re, ssm_c_im, ssm_d, ssm_log_dt, ssm_glu_w,
             w_branch, w_out, ln_mix_g, ln_mix_b, w_xq, w_xkv, w_xo, ln_x_g, ln_x_b, ffn_w_gate_up,
             ffn_w_down, moe_router, moe_w_gate_up, moe_w_down, ln_ffn_g, ln_ffn_b, *, ssm_tc, attn_tq):
    s = x.shape[0]
    tabs = _rope_tables(positions)
    row = lambda v: v[None, :]
    ones_g = jnp.ones((1, 128), F32)
    moe_tm = min(512, s // 4)
    for l in range(DEPTH):
        pk = _pack_layer(l, w_in, w_q_up, w_kv_up, q_norm_g, kv_norm_g, w_branch, w_out, ssm_glu_w,
                         w_xq, w_xkv, w_xo)
        qd, kd, vd, qm, km, vm, su = _inproj(x, pk['wa'], pk['wqup'], pk['wkn'], pk['wv'], pk['qng'],
                                             pk['kvng'], tabs)
        lam_init = 0.8 - 0.6 * math.exp(-0.3 * l)
        lf = diff_lambda[l]
        lam = (jnp.exp(jnp.sum(lf[0] * lf[1])) - jnp.exp(jnp.sum(lf[2] * lf[3])) + lam_init).reshape(1)
        o_diff = _attention(qd, kd, vd, lam, row(diff_subln_g[l]), heads=DIFF_HEADS, dqk=128, diff=True,
                            post_scale=1.0 - lam_init, tq=min(2 * attn_tq, s), tk=min(2 * attn_tq, s))
        o_mla = _attention(qm, km, vm, lam, ones_g, heads=MLA_HEADS, dqk=MLA_QK_PAD, diff=False,
                           tq=min(4 * attn_tq, s), tk=min(2 * attn_tq, s))
        sp = _ssm_params(ssm_lam_re[l], ssm_lam_im[l], ssm_b_re[l], ssm_b_im[l], ssm_c_re[l], ssm_c_im[l],
                         ssm_log_dt[l], s // SSM_SEGS)
        y = _ssm(su, sp, row(ssm_d[l]), ssm_tc)
        o_ssm = _glu(y, pk['wl'], pk['wgl'])
        x = _merge(x, o_diff, o_mla, o_ssm, pk['wgate'], pk['wb'], pk['wo'], row(ln_mix_g[l]), row(ln_mix_b[l]))
        kv = _memkv(mem, pk['wxkv'])
        x = _xattn(x, pk['wxq'], kv, pk['wxo'], row(ln_x_g[l]), row(ln_x_b[l]))
        if l % 2 == 0:
            x = _ffn(x, ffn_w_gate_up[l // 2].astype(BF16), ffn_w_down[l // 2].astype(BF16),
                     row(ln_ffn_g[l]), row(ln_ffn_b[l]))
        else:
            wr = jnp.pad(moe_router[l // 2], ((0, 0), (0, 128 - N_EXPERTS)))
            wts, idx = _router(x, wr)
            src, dest, tile_expert, n_valid = _moe_plan(idx[:, :2], moe_tm)
            y_sorted = _moe_ffn(x, src, tile_expert, n_valid, moe_w_gate_up[l // 2].astype(BF16),
                                moe_w_down[l // 2].astype(BF16), moe_tm)
            x = _moe_combine(x, wts, y_sorted, dest, row(ln_ffn_g[l]), row(ln_ffn_b[l]))
    return x


def kernel(x, mem, positions, w_in, w_q_up, w_kv_up, q_norm_g, kv_norm_g, diff_lambda, diff_subln_g, ssm_lam_re, ssm_lam_im, ssm_b_re, ssm_b_im, ssm_c_re, ssm_c_im, ssm_d, ssm_log_dt, ssm_glu_w, w_branch, w_out, ln_mix_g, ln_mix_b, w_xq, w_xkv, w_xo, ln_x_g, ln_x_b, ffn_w_gate_up, ffn_w_down, moe_router, moe_w_gate_up, moe_w_down, ln_ffn_g, ln_ffn_b):
    assert x.shape[0] == 1 and mem.shape[0] == 1
    seq = x.shape[1]
    out = _forward(x[0], mem[0], positions[0], w_in, w_q_up, w_kv_up, q_norm_g, kv_norm_g, diff_lambda,
                   diff_subln_g, ssm_lam_re, ssm_lam_im, ssm_b_re, ssm_b_im, ssm_c_re, ssm_c_im, ssm_d,
                   ssm_log_dt, ssm_glu_w, w_branch, w_out, ln_mix_g, ln_mix_b, w_xq, w_xkv, w_xo, ln_x_g,
                   ln_x_b, ffn_w_gate_up, ffn_w_down, moe_router, moe_w_gate_up, moe_w_down, ln_ffn_g,
                   ln_ffn_b, ssm_tc=min(256, seq // SSM_SEGS), attn_tq=min(512, seq))
    return out[None]
```

```python
import functools
import math

import jax
import jax.numpy as jnp
from jax import lax
from jax.experimental import pallas as pl
from jax.experimental.pallas import tpu as pltpu

D_MODEL = 2048
DEPTH = 2
MEM_LEN = 256
DIFF_HEADS = 8
DIFF_HEAD_DIM = 64
DIFF_V_DIM = 128
DIFF_ROT_DIM = 16
DIFF_QK_W = 1024
DIFF_V_W = 1024
MLA_HEADS = 8
MLA_Q_RANK = 512
MLA_KV_RANK = 256
MLA_NOPE = 128
MLA_ROPE = 64
MLA_V = 128
MLA_QK_PAD = 256
V_ROWS = 144
ATTN_SLOTS = 3
SSM_CH = 1024
SSM_GROUP = 16
SSM_GROUPS = 64
SSM_STATE = 64
SSM_SEGS = 8
SSM_BLOCKS = 8
SSM_BLOCK_STATE = 512
N_BRANCH = 3
BRANCH_W = 1024
X_HEADS = 4
X_HEAD_DIM = 128
X_W = 512
D_FF = 5632
N_EXPERTS = 8
ROPE_THETA = 500000.0
LN_EPS = 1e-5
RMS_EPS = 1e-6
NEG_BIG = -1e30
ALPHA = (2 * DEPTH) ** 0.25
LOG2E = math.log2(math.e)

A_DQ, A_DK, A_DV, A_MQ, A_MKV, A_SU, A_MKR, A_END = 0, 1024, 2048, 3072, 3584, 3840, 4864, 4992

VMEM_LIMIT = 56 * 1024 * 1024
F32 = jnp.float32
BF16 = jnp.bfloat16
HI = lax.Precision.HIGHEST


def _cparams(sem):
    return pltpu.CompilerParams(dimension_semantics=sem, vmem_limit_bytes=VMEM_LIMIT)


def _layer_norm(z, g, b):
    mu = jnp.mean(z, axis=-1, keepdims=True)
    zc = z - mu
    var = jnp.mean(zc * zc, axis=-1, keepdims=True)
    return zc * lax.rsqrt(var + LN_EPS) * g + b


def _rms_norm(z, g):
    return z * lax.rsqrt(jnp.mean(z * z, axis=-1, keepdims=True) + RMS_EPS) * g


def _dot(a, b):
    return jnp.dot(a, b, preferred_element_type=F32)


def _dot_nt(a, b):
    return lax.dot_general(a, b, (((1,), (1,)), ((), ())), preferred_element_type=F32)


def _inproj_kernel(x_ref, wa_ref, wqup_ref, wkn_ref, wv_ref, qng_ref, kvng_ref,
                   cd_ref, sda_ref, sdb_ref, cm_ref, sma_ref, smb_ref,
                   qd_ref, kd_ref, vd_ref, qm_ref, km_ref, vm_ref, su_ref):
    xb = x_ref[...].astype(BF16)
    h = _dot(xb, wa_ref[...])
    cd, sda, sdb = cd_ref[...], sda_ref[...], sdb_ref[...]
    cm, sma, smb = cm_ref[...], sma_ref[...], smb_ref[...]

    def rope_d(blk):
        return blk * cd + pltpu.roll(blk, 8, 1) * sda + pltpu.roll(blk, 120, 1) * sdb

    def rope_m(blk):
        return blk * cm + pltpu.roll(blk, 32, 1) * sma + pltpu.roll(blk, 96, 1) * smb

    q_scale = DIFF_HEAD_DIM ** -0.5 * LOG2E
    for c in range(DIFF_QK_W // 128):
        lo, hi = c * 128, (c + 1) * 128
        qd_ref[:, lo:hi] = (rope_d(h[:, A_DQ + lo:A_DQ + hi]) * q_scale).astype(BF16)
        kd_ref[:, lo:hi] = rope_d(h[:, A_DK + lo:A_DK + hi]).astype(BF16)
    vd_ref[...] = h[:, A_DV:A_MQ].astype(BF16)
    su_ref[...] = h[:, A_SU:A_MKR]

    cq = _rms_norm(h[:, A_MQ:A_MKV], qng_ref[...]).astype(BF16)
    qm = _dot(cq, wqup_ref[...]) * ((MLA_NOPE + MLA_ROPE) ** -0.5 * LOG2E)
    ckv = _rms_norm(h[:, A_MKV:A_SU], kvng_ref[...]).astype(BF16)
    kn = _dot(ckv, wkn_ref[...])
    vm_ref[...] = _dot(ckv, wv_ref[...]).astype(BF16)
    krot = rope_m(h[:, A_MKR:A_END]).astype(BF16)
    for hh in range(MLA_HEADS):
        b0 = hh * MLA_QK_PAD
        qm_ref[:, b0:b0 + 128] = qm[:, b0:b0 + 128].astype(BF16)
        qm_ref[:, b0 + 128:b0 + 256] = rope_m(qm[:, b0 + 128:b0 + 256]).astype(BF16)
        km_ref[:, b0:b0 + 128] = kn[:, hh * 128:(hh + 1) * 128].astype(BF16)
        km_ref[:, b0 + 128:b0 + 256] = krot


def _inproj(x, wa, wqup, wkn, wv, qng, kvng, tabs, tm=256):
    s = x.shape[0]
    row = lambda w: pl.BlockSpec((tm, w), lambda i: (i, 0))
    const = lambda a: pl.BlockSpec(a.shape, lambda i: (0,) * a.ndim, pipeline_mode=pl.Buffered(1))
    out_shapes = [jax.ShapeDtypeStruct((s, 1024), BF16), jax.ShapeDtypeStruct((s, 1024), BF16),
                  jax.ShapeDtypeStruct((s, 1024), BF16), jax.ShapeDtypeStruct((s, 2048), BF16),
                  jax.ShapeDtypeStruct((s, 2048), BF16), jax.ShapeDtypeStruct((s, 1024), BF16),
                  jax.ShapeDtypeStruct((s // SSM_SEGS, SSM_SEGS * SSM_CH), F32)]
    seg_tiles = s // SSM_SEGS // tm
    su_spec = pl.BlockSpec((tm, SSM_CH), lambda i: (i % seg_tiles, i // seg_tiles))
    return pl.pallas_call(
        _inproj_kernel,
        grid=(s // tm,),
        in_specs=[row(D_MODEL), const(wa), const(wqup), const(wkn), const(wv), const(qng), const(kvng)]
                 + [row(128)] * 6,
        out_specs=[row(1024), row(1024), row(1024), row(2048), row(2048), row(1024), su_spec],
        out_shape=out_shapes,
        compiler_params=_cparams(("parallel",)),
        name="inproj",
    )(x, wa, wqup, wkn, wv, qng, kvng, *tabs)


def _col_reduce(op, s, ways=8):
    parts = s.reshape(ways, s.shape[0] // ways, s.shape[1])
    return op(op(parts, axis=0), axis=0, keepdims=True)


def _attn_kernel(lam_ref, q_ref, k_ref, vt_ref, g_ref, o_ref, q2_sc, m_sc, acc_sc, s_sc, *,
                 tq, tk, cw, diff, post_scale):
    qi = pl.program_id(1)
    qt = q_ref[...].astype(F32).T.astype(BF16)
    if diff:
        feat = lax.broadcasted_iota(jnp.int32, qt.shape, 0)
        zero = jnp.zeros_like(qt)
        q2_sc[:, :tq] = jnp.where(feat < DIFF_HEAD_DIM, qt, zero)
        q2_sc[:, tq:] = jnp.where(feat >= DIFF_HEAD_DIM, qt, zero)
    else:
        q2_sc[...] = qt
    m_sc[...] = jnp.full(m_sc.shape, NEG_BIG, F32)
    acc_sc[...] = jnp.zeros(acc_sc.shape, F32)
    ratio = tq // tk
    cols = q2_sc.shape[1]

    def step(j, diag):
        q0 = lambda c: (c * cw) % tq
        if diag is None:
            units = [(0, tk, c) for c in range(cols // cw)]
            k_full = k_ref[pl.ds(pl.multiple_of(j * tk, tk), tk), :]
            v_full = vt_ref[j]
            keys = lambda ks, kl: k_full
            vals = lambda ks, kl: v_full
        else:
            kl = min(cw, tk)
            units = [(ks, kl, c) for ks in range(0, tk, kl) for c in range(cols // cw)
                     if diag * tk + ks < q0(c) + cw]
            keys = lambda ks, kl: k_ref[pl.ds(pl.multiple_of(j * tk + ks, kl), kl), :]
            vals = lambda ks, kl: vt_ref[j, :, ks:ks + kl]
        nu = len(units)
        cs = lambda c: slice(c * cw, (c + 1) * cw)
        slots = s_sc.shape[0]

        def qk(n):
            ks, kl, c = units[n]
            s = _dot(keys(ks, kl), q2_sc[:, cs(c)])
            if diag is not None:
                kpos = lax.broadcasted_iota(jnp.int32, s.shape, 0) + (diag * tk + ks)
                qpos = lax.broadcasted_iota(jnp.int32, s.shape, 1) + q0(c)
                s = jnp.where(qpos >= kpos, s, NEG_BIG)
            s_sc[n % slots, :kl, :] = s
            return _col_reduce(jnp.max, s)

        def stats(n):
            c = units[n][2]
            m_old = m_sc[:, cs(c)]
            m_new = jnp.maximum(m_old, raw.pop(n))
            m_sc[:, cs(c)] = m_new
            return m_new, jnp.exp2(m_old - m_new)

        def finish(n, m_new, a):
            ks, kl, c = units[n]
            p = jnp.exp2((s_sc[n % slots, :kl, :] - m_new).astype(BF16))
            acc_sc[:, cs(c)] = a * acc_sc[:, cs(c)] + _dot(vals(ks, kl), p)

        ahead = slots - 1
        raw = {n: qk(n) for n in range(min(ahead, nu))}
        st = stats(0)
        for n in range(nu):
            if n + ahead < nu:
                raw[n + ahead] = qk(n + ahead)
            st_next = stats(n + 1) if n + 1 < nu else None
            finish(n, *st)
            st = st_next

    def body(j, carry):
        step(j, None)
        return carry

    lax.fori_loop(0, qi * ratio, body, 0)
    for d in range(ratio):
        step(qi * ratio + d, d)
    ot = acc_sc[:128, :] / acc_sc[128:129, :]
    if diff:
        ot = ot[:, :tq] - lam_ref[0] * ot[:, tq:]
    o = ot.T
    if diff:
        o = _rms_norm(o, g_ref[...]) * post_scale
    o_ref[...] = o.astype(o_ref.dtype)


def _attention(q, k, v, lam, g, *, heads, dqk, diff, post_scale=1.0, tq=512, tk=512, cw=512):
    s = q.shape[0]
    cols = 2 * tq if diff else tq
    nch = s // tk
    vt =v.reshape(nch, tk, heads, 128).transpose(2, 0, 3, 1)
    ones = jnp.ones((heads, nch, V_ROWS - 128, tk), BF16)
    vt = jnp.concatenate([vt, ones], axis=2).reshape(heads * nch, V_ROWS, tk)
    kern = functools.partial(_attn_kernel, tq=tq, tk=tk, cw=cw, diff=diff, post_scale=post_scale)
    return pl.pallas_call(
        kern,
        grid=(heads, s // tq),
        in_specs=[pl.BlockSpec(memory_space=pltpu.SMEM),
                  pl.BlockSpec((tq, dqk), lambda h, i: (i, h)),
                  pl.BlockSpec((s, dqk), lambda h, i: (0, h)),
                  pl.BlockSpec((nch, V_ROWS, tk), lambda h, i: (h, 0, 0)),
                  pl.BlockSpec((1, 128), lambda h, i: (0, 0))],
        out_specs=pl.BlockSpec((tq, 128), lambda h, i: (i, h)),
        out_shape=jax.ShapeDtypeStruct((s, heads * 128), BF16),
        scratch_shapes=[pltpu.VMEM((dqk, cols), BF16), pltpu.VMEM((1, cols), F32),
                        pltpu.VMEM((V_ROWS, cols), F32), pltpu.VMEM((ATTN_SLOTS, tk, cw), F32)],
        compiler_params=_cparams(("parallel", "parallel")),
        name="diff_attn" if diff else "mla_attn",
    )(lam, q, k, vt, g)


def _ssm_kernel(u_ref, wb_ref, ar_ref, ai_ref, wc_ref, d_ref, x0_ref, *refs, tc, emit_y):
    if emit_y:
        y_ref, x_sc, s_sc = refs
    else:
        xe_ref, x_sc, s_sc = refs
    i = pl.program_id(1)
    ns = SSM_BLOCK_STATE

    @pl.when(i == 0)
    def _():
        s_sc[...] = x0_ref[0]

    u = u_ref[...]
    x_sc[...] = _dot(u.astype(BF16), wb_ref[0])
    ar = jnp.broadcast_to(ar_ref[0], (SSM_SEGS, ns))
    ai = jnp.broadcast_to(ai_ref[0], (SSM_SEGS, ns))

    def body(t, carry):
        re, im = carry
        rows = pl.ds(pl.multiple_of(t * SSM_SEGS, SSM_SEGS), SSM_SEGS)
        nre = ar * re - ai * im + x_sc[rows, :ns]
        nim = ar * im + ai * re + x_sc[rows, ns:]
        x_sc[rows, :ns] = nre
        x_sc[rows, ns:] = nim
        return nre, nim

    re, im = lax.fori_loop(0, tc, body, (s_sc[:, :ns], s_sc[:, ns:]), unroll=8)
    s_sc[:, :ns] = re
    s_sc[:, ns:] = im
    if emit_y:
        y = _dot(x_sc[...].astype(BF16), wc_ref[0]) + d_ref[...] * u
        y_ref[...] = jax.nn.gelu(y).astype(y_ref.dtype)
    else:
        @pl.when(i == pl.num_programs(1) - 1)
        def _():
            xe_ref[0] = s_sc[...]


def _ssm_pass(u_perm, p, d, x0, tc, emit_y):
    rows = u_perm.shape[0]
    nchunk = rows // (tc * SSM_SEGS)
    blk3 = lambda a: pl.BlockSpec((1,) + a.shape[1:], lambda j, i: (j, 0, 0))
    if emit_y:
        out_specs = pl.BlockSpec((tc * SSM_SEGS, 128), lambda j, i: (i, j))
        out_shape = jax.ShapeDtypeStruct((rows, SSM_CH), BF16)
    else:
        out_specs = blk3(x0)
        out_shape = jax.ShapeDtypeStruct(x0.shape, F32)
    return pl.pallas_call(
        functools.partial(_ssm_kernel, tc=tc, emit_y=emit_y),
        grid=(SSM_BLOCKS, nchunk),
        in_specs=[pl.BlockSpec((tc * SSM_SEGS, 128), lambda j, i: (i, j)),
                  blk3(p['wb']), blk3(p['ar']), blk3(p['ai']), blk3(p['wc']),
                  pl.BlockSpec((1, 128), lambda j, i: (0, j)), blk3(x0)],
        out_specs=out_specs,
        out_shape=out_shape,
        scratch_shapes=[pltpu.VMEM((tc * SSM_SEGS, 2 * SSM_BLOCK_STATE), F32),
                        pltpu.VMEM((SSM_SEGS, 2 * SSM_BLOCK_STATE), F32)],
        compiler_params=_cparams(("parallel", "arbitrary")),
        name="ssm_scan" if emit_y else "ssm_state",
    )(u_perm, p['wb'], p['ar'], p['ai'], p['wc'], d, x0)


def _ssm_params(lam_re, lam_im, b_re, b_im, c_re, c_im, log_dt, seg_len):
    dt = jnp.exp(log_dt)[:, None]

    def cexp(scale):
        mag = jnp.exp(lam_re * scale)
        return mag * jnp.cos(lam_im * scale), mag * jnp.sin(lam_im * scale)

    ar, ai = cexp(dt)
    sr, si = cexp(dt * seg_len)
    den = lam_re * lam_re + lam_im * lam_im
    fr = ((ar - 1.0) * lam_re + ai * lam_im) / den
    fi = (ai * lam_re - (ar - 1.0) * lam_im) / den
    bbr = fr[..., None] * b_re - fi[..., None] * b_im
    bbi = fr[..., None] * b_im + fi[..., None] * b_re
    eye = jnp.eye(SSM_BLOCKS, dtype=F32)

    def in_mat(w):
        w = w.reshape(SSM_BLOCKS, 8, SSM_STATE, SSM_GROUP)
        return jnp.einsum('jgnp,gh->jgphn', w, eye).reshape(SSM_BLOCKS, 8 * SSM_GROUP, 8 * SSM_STATE)

    def out_mat(w):
        w = w.reshape(SSM_BLOCKS, 8, SSM_GROUP, SSM_STATE)
        return jnp.einsum('jgpn,gh->jgnhp', w, eye).reshape(SSM_BLOCKS, 8 * SSM_STATE, 8 * SSM_GROUP)

    vec = lambda a: a.reshape(SSM_BLOCKS, 1, SSM_BLOCK_STATE)
    return dict(wb=jnp.concatenate([in_mat(bbr), in_mat(bbi)], axis=2).astype(BF16),
                wc=jnp.concatenate([out_mat(c_re), -out_mat(c_im)], axis=1).astype(BF16),
                ar=vec(ar), ai=vec(ai), sr=vec(sr), si=vec(si))


def _ssm(su_seg, p, d, tc):
    seg_len = su_seg.shape[0]
    s = seg_len * SSM_SEGS
    u_perm = su_seg.reshape(s, SSM_CH)
    ns = SSM_BLOCK_STATE
    zeros = jnp.zeros((SSM_BLOCKS, SSM_SEGS, 2 * ns), F32)
    xe = _ssm_pass(u_perm, p, d, zeros, tc, emit_y=False)
    x0r, x0i = [zeros[:, 0, :ns]], [zeros[:, 0, ns:]]
    for sg in range(SSM_SEGS - 1):
        pr, pi = x0r[-1], x0i[-1]
        x0r.append(p['sr'][:, 0] * pr - p['si'][:, 0] * pi + xe[:, sg, :ns])
        x0i.append(p['sr'][:, 0] * pi + p['si'][:, 0] * pr + xe[:, sg, ns:])
    x0 = jnp.concatenate([jnp.stack(x0r, axis=1), jnp.stack(x0i, axis=1)], axis=2)
    y = _ssm_pass(u_perm, p, d, x0, tc, emit_y=True)
    return y.reshape(seg_len, SSM_SEGS, SSM_CH).transpose(1, 0, 2).reshape(s, SSM_CH)


def _glu_kernel(y_ref, wl_ref, wg_ref, o_ref):
    y = y_ref[...]
    o_ref[...] = (_dot(y, wl_ref[...]) * jax.nn.sigmoid(_dot(y, wg_ref[...]))).astype(o_ref.dtype)


def _glu(y, wl, wg, tm=512):
    s = y.shape[0]
    return pl.pallas_call(
        _glu_kernel,
        grid=(s // tm,),
        in_specs=[pl.BlockSpec((tm, SSM_CH), lambda i: (i, 0)),
                  pl.BlockSpec(wl.shape, lambda i: (0, 0)),
                  pl.BlockSpec(wg.shape, lambda i: (0, 0))],
        out_specs=pl.BlockSpec((tm, SSM_CH), lambda i: (i, 0)),
        out_shape=jax.ShapeDtypeStruct((s, SSM_CH), BF16),
        compiler_params=_cparams(("parallel",)),
        name="ssm_glu",
    )(y, wl, wg)


def _merge_kernel(x_ref, od_ref, om_ref, os_ref, wg_ref, wb_ref, wo_ref, g_ref, b_ref, o_ref, xb_sc):
    j = pl.program_id(1)

    @pl.when(j == 0)
    def _():
        xb_sc[...] = x_ref[...].astype(BF16)
        o_ref[...] = jnp.zeros(o_ref.shape, F32)

    xb = xb_sc[...]
    merged = None
    for n, br_ref in enumerate((od_ref, om_ref, os_ref)):
        term = jax.nn.sigmoid(_dot(xb, wg_ref[n])) * _dot(br_ref[...], wb_ref[n])
        merged = term if merged is None else merged + term
    o_ref[...] += _dot(merged.astype(BF16), wo_ref[...])

    @pl.when(j == pl.num_programs(1) - 1)
    def _():
        o_ref[...] = _layer_norm(ALPHA * x_ref[...] + o_ref[...], g_ref[...], b_ref[...])


def _merge(x, od, om, os_, wg, wb, wo, g, b, tm=512, tn=512):
    s = x.shape[0]
    br = pl.BlockSpec((tm, BRANCH_W), lambda i, j: (i, 0))
    vec = pl.BlockSpec((1, D_MODEL), lambda i, j: (0, 0))
    return pl.pallas_call(
        _merge_kernel,
        grid=(s // tm, D_MODEL // tn),
        in_specs=[pl.BlockSpec((tm, D_MODEL), lambda i, j: (i, 0)), br, br, br,
                  pl.BlockSpec((N_BRANCH, D_MODEL, tn), lambda i, j: (0, 0, j)),
                  pl.BlockSpec((N_BRANCH, BRANCH_W, tn), lambda i, j: (0, 0, j)),
                  pl.BlockSpec((tn, D_MODEL), lambda i, j: (j, 0)), vec, vec],
        out_specs=pl.BlockSpec((tm, D_MODEL), lambda i, j: (i, 0)),
        out_shape=jax.ShapeDtypeStruct((s, D_MODEL), F32),
        scratch_shapes=[pltpu.VMEM((tm, D_MODEL), BF16)],
        compiler_params=_cparams(("parallel", "arbitrary")),
        name="merge_out_ln",
    )(x, od, om, os_, wg, wb, wo, g, b)


def _memkv_kernel(m_ref, w_ref, o_ref):
    o_ref[...] = _dot(m_ref[...].astype(BF16), w_ref[...]).astype(o_ref.dtype)


def _memkv(mem, w):
    return pl.pallas_call(
        _memkv_kernel,
        out_shape=jax.ShapeDtypeStruct((mem.shape[0], w.shape[1]), BF16),
        compiler_params=pltpu.CompilerParams(vmem_limit_bytes=VMEM_LIMIT),
        name="mem_kv",
    )(mem, w)


def _xattn_kernel(x_ref, wq_ref, kv_ref, wo_ref, g_ref, b_ref, o_ref):
    x = x_ref[...]
    q = (_dot(x.astype(BF16), wq_ref[...]) * (X_HEAD_DIM ** -0.5)).astype(BF16)
    outs = []
    for h in range(X_HEADS):
        lo, hi = h * X_HEAD_DIM, (h + 1) * X_HEAD_DIM
        s = _dot_nt(q[:, lo:hi], kv_ref[:, lo:hi])
        p = jnp.exp(s - jnp.max(s, axis=1, keepdims=True))
        o = _dot(p.astype(BF16), kv_ref[:, X_W + lo:X_W + hi])
        outs.append((o / jnp.sum(p, axis=1, keepdims=True)).astype(BF16))
    xa = _dot(jnp.concatenate(outs, axis=1), wo_ref[...])
    o_ref[...] = _layer_norm(ALPHA * x + xa, g_ref[...], b_ref[...])


def _xattn(x, wq, kv, wo, g, b, tm=512):
    s = x.shape[0]
    full = lambda a: pl.BlockSpec(a.shape, lambda i: (0, 0))
    return pl.pallas_call(
        _xattn_kernel,
        grid=(s // tm,),
        in_specs=[pl.BlockSpec((tm, D_MODEL), lambda i: (i, 0)), full(wq), full(kv), full(wo), full(g), full(b)],
        out_specs=pl.BlockSpec((tm, D_MODEL), lambda i: (i, 0)),
        out_shape=jax.ShapeDtypeStruct((s, D_MODEL), F32),
        compiler_params=_cparams(("parallel",)),
        name="xattn_ln",
    )(x, wq, kv, wo, g, b)


def _ffn_kernel(x_ref, wg_ref, wu_ref, wd_ref, g_ref, b_ref, o_ref, xb_sc):
    f = pl.program_id(1)

    @pl.when(f == 0)
    def _():
        xb_sc[...] = x_ref[...].astype(BF16)
        o_ref[...] = jnp.zeros(o_ref.shape, F32)

    xb = xb_sc[...]
    a = jax.nn.silu(_dot(xb, wg_ref[...])) * _dot(xb, wu_ref[...])
    o_ref[...] += _dot(a.astype(BF16), wd_ref[...])

    @pl.when(f == pl.num_programs(1) - 1)
    def _():
        o_ref[...] = _layer_norm(ALPHA * x_ref[...] + o_ref[...], g_ref[...], b_ref[...])


def _ffn(x, wgu, wd, g, b, tm=512, tf=512):
    s = x.shape[0]
    nf = D_FF // tf
    vec = pl.BlockSpec((1, D_MODEL), lambda i, f: (0, 0))
    return pl.pallas_call(
        _ffn_kernel,
        grid=(s // tm, nf),
        in_specs=[pl.BlockSpec((tm, D_MODEL), lambda i, f: (i, 0)),
                  pl.BlockSpec((D_MODEL, tf), lambda i, f: (0, f)),
                  pl.BlockSpec((D_MODEL, tf), lambda i, f: (0, nf + f)),
                  pl.BlockSpec((tf, D_MODEL), lambda i, f: (f, 0)), vec, vec],
        out_specs=pl.BlockSpec((tm, D_MODEL), lambda i, f: (i, 0)),
        out_shape=jax.ShapeDtypeStruct((s, D_MODEL), F32),
        scratch_shapes=[pltpu.VMEM((tm, D_MODEL), BF16)],
        compiler_params=_cparams(("parallel", "arbitrary")),
        name="ffn_ln",
    )(x, wgu, wgu, wd, g, b)


def _router_kernel(x_ref, w_ref, wt_ref, ix_ref):
    logits = jnp.dot(x_ref[...], w_ref[...], precision=HI, preferred_element_type=F32)
    lane = lax.broadcasted_iota(jnp.int32, logits.shape, 1)
    logits = jnp.where(lane < N_EXPERTS, logits, NEG_BIG)
    v1 = jnp.max(logits, axis=1, keepdims=True)
    i1 = jnp.min(jnp.where(logits == v1, lane, 128), axis=1, keepdims=True)
    rest = jnp.where(lane == i1, NEG_BIG, logits)
    v2 = jnp.max(rest, axis=1, keepdims=True)
    i2 = jnp.min(jnp.where(rest == v2, lane, 128), axis=1, keepdims=True)
    e2 = jnp.exp(v2 - v1)
    w1 = 1.0 / (1.0 + e2)
    w2 = e2 / (1.0 + e2)
    wt_ref[...] = jnp.where(lane == 0, w1, jnp.where(lane == 1, w2, 0.0))
    ix_ref[...] = jnp.where(lane == 0, i1, jnp.where(lane == 1, i2, 0))


def _router(x, w_pad, tm=512):
    s = x.shape[0]
    out = pl.BlockSpec((tm, 128), lambda i: (i, 0))
    return pl.pallas_call(
        _router_kernel,
        grid=(s // tm,),
        in_specs=[pl.BlockSpec((tm, D_MODEL), lambda i: (i, 0)), pl.BlockSpec(w_pad.shape, lambda i: (0, 0))],
        out_specs=[out, out],
        out_shape=[jax.ShapeDtypeStruct((s, 128), F32), jax.ShapeDtypeStruct((s, 128), jnp.int32)],
        compiler_params=_cparams(("parallel",)),
        name="router_top2",
    )(x, w_pad)


def _moe_plan(expert_idx, tm):
    n_pairs = expert_idx.shape[0] * 2
    n_tiles = n_pairs // tm + N_EXPERTS
    ef = expert_idx.reshape(-1)
    onehot = (ef[:, None] == jnp.arange(N_EXPERTS)[None, :]).astype(jnp.int32)
    rank = jnp.take_along_axis(jnp.cumsum(onehot, axis=0) - onehot, ef[:, None], axis=1)[:, 0]
    padded = (jnp.sum(onehot, axis=0) + tm - 1) // tm * tm
    ends = jnp.cumsum(padded)
    dest = (ends - padded)[ef] + rank
    src = jnp.zeros((n_tiles * tm,), jnp.int32).at[dest].set(jnp.arange(n_pairs, dtype=jnp.int32) // 2)
    n_valid = ends[-1] // tm
    tile_start = jnp.minimum(jnp.arange(n_tiles), n_valid - 1) * tm
    tile_expert = jnp.sum((tile_start[:, None] >= ends[None, :]).astype(jnp.int32), axis=1)
    return src.reshape(n_tiles, 1, tm), dest, tile_expert.astype(jnp.int32), n_valid.reshape(1).astype(jnp.int32)


def _row_copy(src_hbm, row, dst_vmem, r, sem):
    return pltpu.make_async_copy(src_hbm.at[pl.ds(row, 1), :], dst_vmem.at[pl.ds(r, 1), :], sem)


def _gather_start(src_hbm, row_of, dst_vmem, sem, n):
    def start(i, carry):
        _row_copy(src_hbm, row_of(2 * i), dst_vmem, 2 * i, sem).start(priority=0)
        _row_copy(src_hbm, row_of(2 * i + 1), dst_vmem, 2 * i + 1, sem).start(priority=1)
        return carry

    lax.fori_loop(0, n // 2, start, 0, unroll=4)


def _gather_wait(src_hbm, dst_vmem, sem, n):
    def wait(r, carry):
        _row_copy(src_hbm, 0, dst_vmem, r, sem).wait()
        return carry

    lax.fori_loop(0, n, wait, 0, unroll=8)


def _moe_ffn_kernel(te_ref, nv_ref, src_ref, nxt_ref, x_hbm, wg_ref, wu_ref, wd_ref, o_ref,
                    xg_sc, xb_sc, sem, *, tm):
    i, f = pl.program_id(0), pl.program_id(1)
    n_valid = nv_ref[0]
    valid = i < n_valid
    slot = i % 2

    @pl.when(f == 0)
    def _():
        o_ref[...] = jnp.zeros(o_ref.shape, F32)

    @pl.when((i == 0) & (f == 0))
    def _():
        _gather_start(x_hbm, lambda r: src_ref[0, 0, r], xg_sc.at[0], sem.at[0], tm)

    @pl.when(valid & (f == 0))
    def _():
        _gather_wait(x_hbm, xg_sc.at[slot], sem.at[slot], tm)

        @pl.when(i + 1 < n_valid)
        def _():
            _gather_start(x_hbm, lambda r: nxt_ref[0, 0, r], xg_sc.at[1 - slot], sem.at[1 - slot], tm)

        xb_sc[...] = xg_sc[slot].astype(BF16)

    @pl.when(valid)
    def _():
        xb = xb_sc[...]
        a = jax.nn.silu(_dot(xb, wg_ref[0])) * _dot(xb, wu_ref[0])
        o_ref[...] += _dot(a.astype(BF16), wd_ref[0])


def _moe_ffn(x, src, tile_expert, n_valid, wgu, wd, tm, tf=512):
    n_tiles = src.shape[0]
    nf = D_FF // tf
    fcol = lambda i, f, nv: jnp.where(i < nv[0], f, nf - 1)
    grid_spec = pltpu.PrefetchScalarGridSpec(
        num_scalar_prefetch=2,
        grid=(n_tiles, nf),
        in_specs=[pl.BlockSpec((1, 1, tm), lambda i, f, te, nv: (i, 0, 0), memory_space=pltpu.SMEM),
                  pl.BlockSpec((1, 1, tm), lambda i, f, te, nv: (jnp.minimum(i + 1, n_tiles - 1), 0, 0),
                               memory_space=pltpu.SMEM),
                  pl.BlockSpec(memory_space=pl.ANY),
                  pl.BlockSpec((1, D_MODEL, tf), lambda i, f, te, nv: (te[i], 0, fcol(i, f, nv))),
                  pl.BlockSpec((1, D_MODEL, tf), lambda i, f, te, nv: (te[i], 0, nf + fcol(i, f, nv))),
                  pl.BlockSpec((1, tf, D_MODEL), lambda i, f, te, nv: (te[i], fcol(i, f, nv), 0))],
        out_specs=pl.BlockSpec((tm, D_MODEL), lambda i, f, te, nv: (i, 0)),
        scratch_shapes=[pltpu.VMEM((2, tm, D_MODEL), F32), pltpu.VMEM((tm, D_MODEL), BF16),
                        pltpu.SemaphoreType.DMA((2,))])
    return pl.pallas_call(
        functools.partial(_moe_ffn_kernel, tm=tm),
        grid_spec=grid_spec,
        out_shape=jax.ShapeDtypeStruct((n_tiles * tm, D_MODEL), F32),
        compiler_params=_cparams(("arbitrary", "arbitrary")),
        name="moe_ffn",
    )(tile_expert, n_valid, src, src, x, wgu, wgu, wd)


def _moe_combine_kernel(p1_ref, p2_ref, x_ref, wt_ref, y_hbm, g_ref, b_ref, o_ref, y1_sc, y2_sc, sem, *, tm):
    _gather_start(y_hbm, lambda r: p1_ref[0, 0, r], y1_sc, sem.at[0], tm)
    _gather_start(y_hbm, lambda r: p2_ref[0, 0, r], y2_sc, sem.at[1], tm)
    _gather_wait(y_hbm, y1_sc, sem.at[0], tm)
    _gather_wait(y_hbm, y2_sc, sem.at[1], tm)
    wt = wt_ref[...]
    f = wt[:, 0:1] * y1_sc[...] + wt[:, 1:2] * y2_sc[...]
    o_ref[...] = _layer_norm(ALPHA * x_ref[...] + f, g_ref[...], b_ref[...])


def _moe_combine(x, wts, y_sorted, dest, g, b, tm=256):
    s = x.shape[0]
    pos = dest.reshape(s // tm, tm, 2)
    p1, p2 = pos[:, None, :, 0], pos[:, None, :, 1]
    smem = pl.BlockSpec((1, 1, tm), lambda i: (i, 0, 0), memory_space=pltpu.SMEM)
    vec = pl.BlockSpec((1, D_MODEL), lambda i: (0, 0))
    return pl.pallas_call(
        functools.partial(_moe_combine_kernel, tm=tm),
        grid=(s // tm,),
        in_specs=[smem, smem, pl.BlockSpec((tm, D_MODEL), lambda i: (i, 0)),
                  pl.BlockSpec((tm, 128), lambda i: (i, 0)), pl.BlockSpec(memory_space=pl.ANY), vec, vec],
        out_specs=pl.BlockSpec((tm, D_MODEL), lambda i: (i, 0)),
        out_shape=jax.ShapeDtypeStruct((s, D_MODEL), F32),
        scratch_shapes=[pltpu.VMEM((tm, D_MODEL), F32), pltpu.VMEM((tm, D_MODEL), F32),
                        pltpu.SemaphoreType.DMA((2,))],
        compiler_params=_cparams(("arbitrary",)),
        name="moe_combine_ln",
    )(p1, p2, x, wts, y_sorted, g, b)


def _rope_tables(positions):
    pos = positions.astype(F32)[:, None]
    lane = jnp.arange(128) % 64

    def tables(rot):
        half = rot // 2
        inv_freq = jnp.power(jnp.float32(ROPE_THETA), -jnp.arange(0, rot, 2, dtype=F32) / rot)
        ang = pos * inv_freq[None, :]
        cos = jnp.take(jnp.cos(ang), lane % half, axis=1)
        sin = jnp.take(jnp.sin(ang), lane % half, axis=1)
        first, second = (lane < half)[None, :], ((lane >= half) & (lane < rot))[None, :]
        c = jnp.where(first | second, cos, 1.0)
        s_prev = jnp.where(second, sin, 0.0)
        s_next = jnp.where(first, -sin, 0.0)
        return c, s_prev, s_next

    return tables(DIFF_ROT_DIM) + tables(MLA_ROPE)


def _pack_layer(l, w_in, w_q_up, w_kv_up, q_norm_g, kv_norm_g, w_branch, w_out, ssm_glu_w, w_xq, w_xkv, w_xo):
    wi = w_in[l]
    parts = [wi[:, 0:3072], wi[:, 3072:3584], wi[:, 3584:3840], wi[:, 3904:4928], wi[:, 3840:3904],
             jnp.zeros((D_MODEL, 64), F32)]
    wa = jnp.concatenate(parts, axis=1).astype(BF16)
    wqup = w_q_up[l].reshape(MLA_Q_RANK, MLA_HEADS, MLA_NOPE + MLA_ROPE)
    wqup = jnp.pad(wqup, ((0, 0), (0, 0), (0, MLA_QK_PAD - MLA_NOPE - MLA_ROPE)))
    wqup = wqup.reshape(MLA_Q_RANK, MLA_HEADS * MLA_QK_PAD).astype(BF16)
    wkv = w_kv_up[l].reshape(MLA_KV_RANK, MLA_HEADS, MLA_NOPE + MLA_V)
    wkn = wkv[:, :, :MLA_NOPE].reshape(MLA_KV_RANK, MLA_HEADS * MLA_NOPE).astype(BF16)
    wv = wkv[:, :, MLA_NOPE:].reshape(MLA_KV_RANK, MLA_HEADS * MLA_V).astype(BF16)
    wgate = wi[:, 4928:].reshape(D_MODEL, N_BRANCH, D_MODEL).transpose(1, 0, 2).astype(BF16)
    return dict(wa=wa, wqup=wqup, wkn=wkn, wv=wv,
                qng=q_norm_g[l][None, :], kvng=kv_norm_g[l][None, :],
                wgate=wgate, wb=w_branch[l].astype(BF16), wo=w_out[l].astype(BF16),
                wl=ssm_glu_w[l][:, :SSM_CH].astype(BF16), wgl=ssm_glu_w[l][:, SSM_CH:].astype(BF16),
                wxq=w_xq[l].astype(BF16), wxkv=w_xkv[l].astype(BF16), wxo=w_xo[l].astype(BF16))


def _forward(x, mem, positions, w_in, w_q_up, w_kv_up, q_norm_g, kv_norm_g, diff_lambda, diff_subln_g,
             ssm_lam_re, ssm_lam_im, ssm_b_re, ssm_b_im, ssm_c_re, ssm_c_im, ssm_d, ssm_log_dt, ssm_glu_w,
             w_branch, w_out, ln_mix_g, ln_mix_b, w_xq, w_xkv, w_xo, ln_x_g, ln_x_b, ffn_w_gate_up,
             ffn_w_down, moe_router, moe_w_gate_up, moe_w_down, ln_ffn_g, ln_ffn_b, *, ssm_tc, attn_tq):
    s = x.shape[0]
    tabs = _rope_tables(positions)
    row = lambda v: v[None, :]
    ones_g = jnp.ones((1, 128), F32)
    moe_tm = min(512, s // 4)
    for l in range(DEPTH):
        pk = _pack_layer(l, w_in, w_q_up, w_kv_up, q_norm_g, kv_norm_g, w_branch, w_out, ssm_glu_w,
                         w_xq, w_xkv, w_xo)
        qd, kd, vd, qm, km, vm, su = _inproj(x, pk['wa'], pk['wqup'], pk['wkn'], pk['wv'], pk['qng'],
                                             pk['kvng'], tabs)
        lam_init = 0.8 - 0.6 * math.exp(-0.3 * l)
        lf = diff_lambda[l]
        lam = (jnp.exp(jnp.sum(lf[0] * lf[1])) - jnp.exp(jnp.sum(lf[2] * lf[3])) + lam_init).reshape(1)
        o_diff = _attention(qd, kd, vd, lam, row(diff_subln_g[l]), heads=DIFF_HEADS, dqk=128, diff=True,
                            post_scale=1.0 - lam_init, tq=min(2 * attn_tq, s), tk=min(2 * attn_tq, s))
        o_mla = _attention(qm, km, vm, lam, ones_g, heads=MLA_HEADS, dqk=MLA_QK_PAD, diff=False,
                           tq=min(4 * attn_tq, s), tk=min(2 * attn_tq, s))
        sp = _ssm_params(ssm_lam_re[l], ssm_lam_im[l], ssm_b_re[l], ssm_b_im[l], ssm_c_re[l], ssm_c_im[l],
                         ssm_log_dt[l], s // SSM_SEGS)
        y = _ssm(su, sp, row(ssm_d[l]), ssm_tc)
        o_ssm = _glu(y, pk['wl'], pk['wgl'])
        x = _merge(x, o_diff, o_mla, o_ssm, pk['wgate'], pk['wb'], pk['wo'], row(ln_mix_g[l]), row(ln_mix_b[l]))
        kv = _memkv(mem, pk['wxkv'])
        x = _xattn(x, pk['wxq'], kv, pk['wxo'], row(ln_x_g[l]), row(ln_x_b[l]))
        if l % 2 == 0:
            x = _ffn(x, ffn_w_gate_up[l // 2].astype(BF16), ffn_w_down[l // 2].astype(BF16),
                     row(ln_ffn_g[l]), row(ln_ffn_b[l]))
        else:
            wr = jnp.pad(moe_router[l // 2], ((0, 0), (0, 128 - N_EXPERTS)))
            wts, idx = _router(x, wr)
            src, dest, tile_expert, n_valid = _moe_plan(idx[:, :2], moe_tm)
            y_sorted = _moe_ffn(x, src, tile_expert, n_valid, moe_w_gate_up[l // 2].astype(BF16),
                                moe_w_down[l // 2].astype(BF16), moe_tm)
            x = _moe_combine(x, wts, y_sorted, dest, row(ln_ffn_g[l]), row(ln_ffn_b[l]))
    return x


def kernel(x, mem, positions, w_in, w_q_up, w_kv_up, q_norm_g, kv_norm_g, diff_lambda, diff_subln_g, ssm_lam_re, ssm_lam_im, ssm_b_re, ssm_b_im, ssm_c_re, ssm_c_im, ssm_d, ssm_log_dt, ssm_glu_w, w_branch, w_out, ln_mix_g, ln_mix_b, w_xq, w_xkv, w_xo, ln_x_g, ln_x_b, ffn_w_gate_up, ffn_w_down, moe_router, moe_w_gate_up, moe_w_down, ln_ffn_g, ln_ffn_b):
    assert x.shape[0] == 1 and mem.shape[0] == 1
    seq = x.shape[1]
    out = _forward(x[0], mem[0], positions[0], w_in, w_q_up, w_kv_up, q_norm_g, kv_norm_g, diff_lambda,
                   diff_subln_g, ssm_lam_re, ssm_lam_im, ssm_b_re, ssm_b_im, ssm_c_re, ssm_c_im, ssm_d,
                   ssm_log_dt, ssm_glu_w, w_branch, w_out, ln_mix_g, ln_mix_b, w_xq, w_xkv, w_xo, ln_x_g,
                   ln_x_b, ffn_w_gate_up, ffn_w_down, moe_router, moe_w_gate_up, moe_w_down, ln_ffn_g,
                   ln_ffn_b, ssm_tc=min(256, seq // SSM_SEGS), attn_tq=min(512, seq))
    return out[None]
```
